```python
import jax, jax.numpy as jnp
from jax import lax
import numpy as np

D_MODEL = 1024
BATCH = 8
SEQ = 4096
DEPTH = 1

CHUNK = 64
EPS = 1e-6
ROPE_THETA = 10000.0
GLA_HEADS = 4
GLA_DK = 64
GLA_DV = 128
GLA_GATE_RANK = 16
GLA_TAU = 16.0
DSA_HEADS = 4
DSA_HEAD_DIM = 128
IDX_HEADS = 4
IDX_DIM = 64
TOPK_MAX = 256
Q_BLOCK = 128
D_FF = 4 * D_MODEL

GLA_QK = GLA_HEADS * GLA_DK
GLA_V = GLA_HEADS * GLA_DV
DSA_Q = DSA_HEADS * DSA_HEAD_DIM
IDX_Q = IDX_HEADS * IDX_DIM
SPLITS = (GLA_QK, GLA_QK, GLA_V, GLA_GATE_RANK, GLA_V, DSA_Q, DSA_HEAD_DIM, DSA_HEAD_DIM, IDX_Q, IDX_DIM, IDX_HEADS)
D_IN = sum(SPLITS)
D_MIX = GLA_V + DSA_Q

kernel_name = 'hymba_gla_dsa_sandwich_block'


def rmsnorm(x, w):
    xf = x.astype(jnp.float32)
    y = xf * lax.rsqrt(jnp.mean(xf * xf, axis=-1, keepdims=True) + EPS)
    return (y * w.astype(jnp.float32)).astype(x.dtype)


def layernorm(x, w, b):
    xf = x.astype(jnp.float32)
    mu = jnp.mean(xf, axis=-1, keepdims=True)
    var = jnp.mean(jnp.square(xf - mu), axis=-1, keepdims=True)
    y = (xf - mu) * lax.rsqrt(var + EPS)
    return (y * w.astype(jnp.float32) + b.astype(jnp.float32)).astype(x.dtype)


def rope(x, pos):
    d = x.shape[-1]
    inv = ROPE_THETA ** (-jnp.arange(0, d, 2, dtype=jnp.float32) / d)
    ang = pos.astype(jnp.float32)[:, None] * inv[None, :]
    cos = jnp.cos(ang)[None, :, None, :]
    sin = jnp.sin(ang)[None, :, None, :]
    x1, x2 = jnp.split(x.astype(jnp.float32), 2, axis=-1)
    out = jnp.concatenate([x1 * cos - x2 * sin, x2 * cos + x1 * sin], axis=-1)
    return out.astype(x.dtype)


def gla_mixer(q, k, v, g_log, r, norm_w):
    B, S = q.shape[0], q.shape[1]
    N = S // CHUNK

    def to_chunks(t, d):
        return t.astype(jnp.float32).reshape(B, N, CHUNK, GLA_HEADS, d).transpose(0, 3, 1, 2, 4)

    qc = to_chunks(q, GLA_DK) * (GLA_DK ** -0.5)
    kc = to_chunks(k, GLA_DK)
    vc = to_chunks(v, GLA_DV)
    gc = to_chunks(g_log, GLA_DK)
    bcum = jnp.cumsum(gc, axis=3)
    b_last = bcum[..., -1:, :]
    q_dec = qc * jnp.exp(bcum)
    k_inv = kc * jnp.exp(-bcum)
    k_end = kc * jnp.exp(b_last - bcum)
    causal = jnp.tril(jnp.ones((CHUNK, CHUNK), dtype=bool))
    attn = jnp.where(causal, jnp.einsum('bhnid,bhnjd->bhnij', q_dec, k_inv), 0.0)
    o_intra = jnp.einsum('bhnij,bhnjv->bhniv', attn, vc)
    u = jnp.einsum('bhnjd,bhnjv->nbhdv', k_end, vc)
    decay = jnp.exp(b_last[..., 0, :]).transpose(2, 0, 1, 3)

    def step(state, inp):
        dec, uc = inp
        return dec[..., None] * state + uc, state

    s0 = jnp.zeros(u.shape[1:], jnp.float32)
    _, s_in = lax.scan(step, s0, (decay, u))
    o_inter = jnp.einsum('bhnid,nbhdv->bhniv', q_dec, s_in)
    o = (o_intra + o_inter).transpose(0, 2, 3, 1, 4).reshape(B, S, GLA_HEADS, GLA_DV)
    o = rmsnorm(o, norm_w).reshape(B, S, GLA_V) * jax.nn.silu(r.astype(jnp.float32))
    return o


def dsa_mixer(q, k, v, qi, ki, wi):
    B, S = q.shape[0], q.shape[1]
    n_sel = min(TOPK_MAX, S // 4)
    nb = S // Q_BLOCK
    key_chunk = jnp.arange(S) // CHUNK
    kf = ki.astype(jnp.float32)

    def blocks(t):
        return t.reshape((B, nb, Q_BLOCK) + t.shape[2:]).swapaxes(0, 1)

    def one_block(args):
        qb, qib, wib, qchunk = args
        allowed = key_chunk[None, :] <= qchunk[:, None]
        idx_logits = jnp.einsum('bqhd,bsd->bqhs', qib.astype(jnp.float32), kf)
        score = jnp.einsum('bqh,bqhs->bqs', wib.astype(jnp.float32), jax.nn.relu(idx_logits))
        score = jnp.where(allowed[None], score, -jnp.inf)
        _, sel = lax.top_k(score, n_sel)
        valid = key_chunk[sel] <= qchunk[None, :, None]
        k_sel = jax.vmap(lambda kb, ib: kb[ib])(k, sel)
        v_sel = jax.vmap(lambda vb, ib: vb[ib])(v, sel)
        logits = jnp.einsum('bqhd,bqkd->bqhk', qb.astype(jnp.float32), k_sel.astype(jnp.float32)) * (DSA_HEAD_DIM ** -0.5)
        logits = jnp.where(valid[:, :, None, :], logits, -jnp.inf)
        p = jax.nn.softmax(logits, axis=-1)
        return jnp.einsum('bqhk,bqkd->bqhd', p, v_sel.astype(jnp.float32))

    out = lax.map(one_block, (blocks(q), blocks(qi), blocks(wi), key_chunk.reshape(nb, Q_BLOCK)))
    return out.swapaxes(0, 1).reshape(B, S, DSA_Q)


def setup_inputs(seed: int = 0) -> dict:
    key = jax.random.key(seed)
    ks = jax.random.split(key, 16)
    f32 = jnp.float32

    def gain(k, shape):
        return 1.0 + 0.02 * jax.random.normal(k, shape, f32)

    return {
        'x': jax.random.normal(ks[0], (BATCH, SEQ, D_MODEL), f32),
        'norm_mix_pre': gain(ks[1], (DEPTH, D_MODEL)),
        'w_in': jax.random.normal(ks[2], (DEPTH, D_MODEL, D_IN), f32) * D_MODEL ** -0.5,
        'gla_gate_w2': jax.random.normal(ks[3], (DEPTH, GLA_GATE_RANK, GLA_QK), f32) * GLA_GATE_RANK ** -0.5,
        'gla_gate_b': 0.1 * jax.random.normal(ks[4], (DEPTH, GLA_QK), f32),
        'gla_norm_w': gain(ks[5], (DEPTH, GLA_DV)),
        'idx_k_norm_w': gain(ks[6], (DEPTH, IDX_DIM)),
        'idx_k_norm_b': 0.02 * jax.random.normal(ks[7], (DEPTH, IDX_DIM), f32),
        'w_out': jax.random.normal(ks[8], (DEPTH, D_MIX, D_MODEL), f32) * D_MIX ** -0.5,
        'norm_mix_post': gain(ks[9], (DEPTH, D_MODEL)),
        'norm_ffn_pre': gain(ks[10], (DEPTH, D_MODEL)),
        'w_ff1': jax.random.normal(ks[11], (DEPTH, D_MODEL, D_FF), f32) * D_MODEL ** -0.5,
        'w_ff2': jax.random.normal(ks[12], (DEPTH, D_FF, D_MODEL), f32) * D_FF ** -0.5,
        'norm_ffn_post': gain(ks[13], (DEPTH, D_MODEL)),
    }


def reference(x, norm_mix_pre, w_in, gla_gate_w2, gla_gate_b, gla_norm_w, idx_k_norm_w, idx_k_norm_b,
              w_out, norm_mix_post, norm_ffn_pre, w_ff1, w_ff2, norm_ffn_post):
    B, S = x.shape[0], x.shape[1]
    pos = jnp.arange(S)
    offsets = [int(o) for o in np.cumsum(SPLITS)[:-1]]
    for l in range(DEPTH):
        h = rmsnorm(x, norm_mix_pre[l])
        proj = h @ w_in[l]
        (g_q, g_k, g_v, g_lr, g_r, d_q, d_k, d_v, i_q, i_k, i_w) = jnp.split(proj, offsets, axis=-1)
        g_log = jax.nn.log_sigmoid((g_lr @ gla_gate_w2[l] + gla_gate_b[l]).astype(jnp.float32)) / GLA_TAU
        gla_out = gla_mixer(g_q.reshape(B, S, GLA_HEADS, GLA_DK), g_k.reshape(B, S, GLA_HEADS, GLA_DK), g_v,
                            g_log.reshape(B, S, GLA_HEADS, GLA_DK), g_r, gla_norm_w[l])
        d_q = rope(d_q.reshape(B, S, DSA_HEADS, DSA_HEAD_DIM), pos)
        d_k = rope(d_k[:, :, None, :], pos)[:, :, 0, :]
        i_q = rope(i_q.reshape(B, S, IDX_HEADS, IDX_DIM), pos)
        i_k = rope(layernorm(i_k, idx_k_norm_w[l], idx_k_norm_b[l])[:, :, None, :], pos)[:, :, 0, :]
        i_w = i_w * (IDX_HEADS ** -0.5 * IDX_DIM ** -0.5)
        dsa_out = dsa_mixer(d_q, d_k, d_v, i_q, i_k, i_w)
        mixed = jnp.concatenate([gla_out, dsa_out], axis=-1).astype(x.dtype) @ w_out[l]
        x = x + rmsnorm(mixed, norm_mix_post[l])
        hf = rmsnorm(x, norm_ffn_pre[l]) @ w_ff1[l]
        ff = jnp.square(jax.nn.relu(hf)) @ w_ff2[l]
        x = x + rmsnorm(ff, norm_ffn_post[l])
    return x
```

```python
import functools

import jax
import jax.numpy as jnp
import numpy as np
from jax import lax
from jax.experimental import pallas as pl
from jax.experimental.pallas import tpu as pltpu

F32 = jnp.float32
BF16 = jnp.bfloat16

EPS = 1e-6
ROPE_THETA = 10000.0
CHUNK = 64
GLA_HEADS = 4
GLA_DK = 64
GLA_DV = 128
GLA_GATE_RANK = 16
GLA_TAU = 16.0
DSA_HEADS = 4
DSA_HEAD_DIM = 128
IDX_HEADS = 4
IDX_DIM = 64
TOPK_MAX = 256

GLA_QK = GLA_HEADS * GLA_DK
GLA_V = GLA_HEADS * GLA_DV
DSA_Q = DSA_HEADS * DSA_HEAD_DIM
IDX_Q = IDX_HEADS * IDX_DIM

LANES = 128
VMEM_LIMIT = 56 * 1024 * 1024

C_GQ = 0
C_GK = C_GQ + GLA_QK
C_GV = C_GK + GLA_QK
C_GR = C_GV + GLA_V
C_DQ = C_GR + GLA_V
C_DK = C_DQ + DSA_Q
C_DV = C_DK + DSA_HEAD_DIM
C_IQ = C_DV + DSA_HEAD_DIM
C_MISC = C_IQ + IDX_Q
D_PACK = C_MISC + LANES
M_IW = IDX_DIM
M_LR = IDX_DIM + IDX_HEADS

PROJ_TM = 512
GLA_TS = 512
DSA_QB = 128
DSA_TK = 256
DSA_BISECT = 14
FFN_TM = 512
FFN_TF = 1024


def _rms(x, w):
    return x * lax.rsqrt(jnp.mean(x * x, axis=-1, keepdims=True) + EPS) * w


def _proj_kernel(x_ref, nw_ref, w_ref, w2_ref, gb_ref, lnw_ref, lnb_ref,
                 cos128_ref, sin128_ref, cos64_ref, sin64_ref,
                 gq_ref, gk_ref, gv_ref, gr_ref, glog_ref,
                 dq_ref, dk_ref, dv_ref, iq_ref, ik_ref, misc_ref):
    hb = _rms(x_ref[...], nw_ref[...]).astype(BF16)

    def proj(lo, width):
        return jnp.dot(hb, w_ref[:, lo:lo + width], preferred_element_type=F32)

    gq_ref[...] = proj(C_GQ, GLA_QK)
    gk_ref[...] = proj(C_GK, GLA_QK)
    gv_ref[...] = proj(C_GV, GLA_V).astype(BF16)
    gr_ref[...] = proj(C_GR, GLA_V)
    dv_ref[...] = proj(C_DV, DSA_HEAD_DIM).astype(BF16)

    cos128 = cos128_ref[...]
    sin128 = sin128_ref[...]
    cos64 = cos64_ref[...]
    sin64 = sin64_ref[...]
    lane = lax.broadcasted_iota(jnp.int32, (1, LANES), 1)
    first_half64 = (lane % IDX_DIM) < (IDX_DIM // 2)

    def rope128(t):
        return t * cos128 + pltpu.roll(t, LANES // 2, 1) * sin128

    def rope64(t):
        rot = jnp.where(first_half64, pltpu.roll(t, LANES - IDX_DIM // 2, 1), pltpu.roll(t, IDX_DIM // 2, 1))
        return t * cos64 + rot * sin64

    dq = proj(C_DQ, DSA_Q)
    for h in range(DSA_HEADS):
        sl = slice(h * LANES, (h + 1) * LANES)
        dq_ref[:, sl] = rope128(dq[:, sl]).astype(BF16)
    dk_ref[...] = rope128(proj(C_DK, DSA_HEAD_DIM)).astype(BF16)

    iq = proj(C_IQ, IDX_Q)
    for c in range(IDX_Q // LANES):
        sl = slice(c * LANES, (c + 1) * LANES)
        iq_ref[:, sl] = rope64(iq[:, sl]).astype(BF16)

    misc = proj(C_MISC, LANES)
    misc_ref[...] = misc * (IDX_HEADS ** -0.5 * IDX_DIM ** -0.5)

    is_ik = lane < IDX_DIM
    mu = jnp.sum(jnp.where(is_ik, misc, 0.0), axis=-1, keepdims=True) * (1.0 / IDX_DIM)
    d = jnp.where(is_ik, misc - mu, 0.0)
    var = jnp.sum(d * d, axis=-1, keepdims=True) * (1.0 / IDX_DIM)
    y = d * lax.rsqrt(var + EPS) * lnw_ref[...] + lnb_ref[...]
    yr = rope64(y)
    ik_ref[...] = (yr + pltpu.roll(yr, LANES // 2, 1)).astype(BF16)

    z = jnp.dot(misc.astype(BF16), w2_ref[...], preferred_element_type=F32) + gb_ref[...]
    glog_ref[...] = (jnp.minimum(z, 0.0) - jnp.log1p(jnp.exp(-jnp.abs(z)))) * (1.0 / GLA_TAU)


def _projection(x2, nw, w_pack, w2_pad, gb, lnw, lnb, tabs, seq):
    t, d = x2.shape
    tm = PROJ_TM
    assert t % tm == 0 and seq % tm == 0
    tiles_per_seq = seq // tm

    def row(i):
        return (i, 0)

    def const(i):
        return (0, 0)

    def pos(i):
        return (i % tiles_per_seq, 0)

    widths = [(GLA_QK, F32), (GLA_QK, F32), (GLA_V, BF16), (GLA_V, F32), (GLA_QK, F32),
              (DSA_Q, BF16), (DSA_HEAD_DIM, BF16), (DSA_HEAD_DIM, BF16), (IDX_Q, BF16),
              (LANES, BF16), (LANES, F32)]
    return pl.pallas_call(
        _proj_kernel,
        grid=(t // tm,),
        in_specs=[
            pl.BlockSpec((tm, d), row),
            pl.BlockSpec((1, d), const),
            pl.BlockSpec((d, D_PACK), const),
            pl.BlockSpec((LANES, GLA_QK), const),
            pl.BlockSpec((1, GLA_QK), const),
            pl.BlockSpec((1, LANES), const),
            pl.BlockSpec((1, LANES), const),
            pl.BlockSpec((tm, LANES), pos),
            pl.BlockSpec((tm, LANES), pos),
            pl.BlockSpec((tm, LANES), pos),
            pl.BlockSpec((tm, LANES), pos),
        ],
        out_specs=[pl.BlockSpec((tm, w), row) for w, _ in widths],
        out_shape=[jax.ShapeDtypeStruct((t, w), dt) for w, dt in widths],
        compiler_params=pltpu.CompilerParams(dimension_semantics=("parallel",), vmem_limit_bytes=VMEM_LIMIT),
        name="proj",
    )(x2, nw, w_pack, w2_pad, gb, lnw, lnb, *tabs)


def _gla_kernel(q_ref, k_ref, g_ref, v_ref, r_ref, nw_ref, o_ref, st_ref):
    @pl.when(pl.program_id(1) == 0)
    def _():
        st_ref[...] = jnp.zeros_like(st_ref)

    ts = q_ref.shape[0]
    n_pairs = GLA_HEADS // 2
    ri = lax.broadcasted_iota(jnp.int32, (CHUNK, CHUNK), 0)
    ci = lax.broadcasted_iota(jnp.int32, (CHUNK, CHUNK), 1)
    causal = ri >= ci
    tri = jnp.where(causal, 1.0, 0.0).astype(BF16)
    lane = lax.broadcasted_iota(jnp.int32, (1, LANES), 1)
    half_mask = [lane < GLA_DK, lane >= GLA_DK]
    nw = nw_ref[...]
    nt = (((1,), (1,)), ((), ()))
    tn = (((0,), (0,)), ((), ()))

    for c in range(ts // CHUNK):
        rows = slice(c * CHUNK, (c + 1) * CHUNK)
        g = g_ref[rows, :]
        g1 = g.astype(BF16)
        g2 = (g - g1.astype(F32)).astype(BF16)
        g3 = (g - g1.astype(F32) - g2.astype(F32)).astype(BF16)
        bcum = (jnp.dot(tri, g1, preferred_element_type=F32)
                + jnp.dot(tri, g2, preferred_element_type=F32)
                + jnp.dot(tri, g3, preferred_element_type=F32))
        b_last = bcum[CHUNK - 1:CHUNK, :]
        q_dec = q_ref[rows, :] * (GLA_DK ** -0.5) * jnp.exp(bcum)
        k = k_ref[rows, :]
        k_inv = (k * jnp.exp(-bcum)).astype(BF16)
        k_end = k * jnp.exp(b_last - bcum)
        decay = jnp.exp(b_last)
        for p in range(n_pairs):
            lanes = slice(p * LANES, (p + 1) * LANES)
            qd = q_dec[:, lanes]
            ki = k_inv[:, lanes]
            ke = k_end[:, lanes]
            st = st_ref[p]
            st_b = st.astype(BF16)
            ke_parts = []
            v_parts = []
            for e in range(2):
                h = 2 * p + e
                vh = v_ref[rows, h * GLA_DV:(h + 1) * GLA_DV]
                qm = jnp.where(half_mask[e], qd, 0.0).astype(BF16)
                attn = lax.dot_general(qm, ki, nt, preferred_element_type=F32)
                attn = jnp.where(causal, attn, 0.0).astype(BF16)
                o = (jnp.dot(attn, vh, preferred_element_type=F32)
                     + lax.dot_general(qm, st_b, nt, preferred_element_type=F32))
                o = _rms(o, nw)
                r = r_ref[rows, h * GLA_DV:(h + 1) * GLA_DV]
                o_ref[rows, h * GLA_DV:(h + 1) * GLA_DV] = (o * (r * jax.nn.sigmoid(r))).astype(BF16)
                ke_parts.append(jnp.where(half_mask[e], ke, 0.0).astype(BF16))
                v_parts.append(vh)
            upd = lax.dot_general(jnp.concatenate(v_parts, axis=0), jnp.concatenate(ke_parts, axis=0), tn,
                                  preferred_element_type=F32)
            st_ref[p] = st * decay[:, lanes] + upd


def _gla(gq, gk, glog, gv, gr, nw, batch, seq):
    ts = GLA_TS
    assert seq % ts == 0 and ts % CHUNK == 0
    n = seq // ts

    def row(b, s):
        return (b * n + s, 0)

    return pl.pallas_call(
        _gla_kernel,
        grid=(batch, n),
        in_specs=[
            pl.BlockSpec((ts, GLA_QK), row),
            pl.BlockSpec((ts, GLA_QK), row),
            pl.BlockSpec((ts, GLA_QK), row),
            pl.BlockSpec((ts, GLA_V), row),
            pl.BlockSpec((ts, GLA_V), row),
            pl.BlockSpec((1, GLA_DV), lambda b, s: (0, 0)),
        ],
        out_specs=pl.BlockSpec((ts, GLA_V), row),
        out_shape=jax.ShapeDtypeStruct((batch * seq, GLA_V), BF16),
        scratch_shapes=[pltpu.VMEM((GLA_HEADS // 2, GLA_DV, 2 * GLA_DK), F32)],
        compiler_params=pltpu.CompilerParams(dimension_semantics=("parallel", "arbitrary"),
                                             vmem_limit_bytes=VMEM_LIMIT),
        name="gla",
    )(gq, gk, glog, gv, gr, nw)


def _dsa_kernel(qi_ref, misc_ref, q_ref, ik_ref, k_ref, v_ref, o_ref, sc_ref, *, n_sel):
    qb = qi_ref.shape[0]
    tk = sc_ref.shape[2]
    j = pl.program_id(1)
    n_tiles = ((j + 1) * qb + tk - 1) // tk
    nt = (((1,), (1,)), ((), ()))

    row = lax.broadcasted_iota(jnp.int32, (qb, 1), 0)
    limit = j * qb + (row // CHUNK + 1) * CHUNK
    k_eff = jnp.minimum(limit, n_sel).astype(F32)
    lane = lax.broadcasted_iota(jnp.int32, (1, LANES), 1)
    kpos = lax.broadcasted_iota(jnp.int32, (1, tk), 1)

    misc = misc_ref[...]
    qi = qi_ref[...]
    qm = []
    wh = []
    for h in range(IDX_HEADS):
        pair = qi[:, (h // 2) * LANES:(h // 2 + 1) * LANES].astype(F32)
        in_head = (lane // IDX_DIM) == (h % 2)
        qm.append(jnp.where(in_head, pair, 0.0).astype(BF16))
        wh.append(jnp.sum(jnp.where(lane == M_IW + h, misc, 0.0), axis=-1, keepdims=True))

    def score_tile(t, carry):
        vmin, vmax = carry
        kt = ik_ref[pl.ds(pl.multiple_of(t * tk, tk), tk), :]
        sc = jnp.zeros((qb, tk), F32)
        for h in range(IDX_HEADS):
            s = lax.dot_general(qm[h], kt, nt, preferred_element_type=F32)
            sc = sc + wh[h] * jnp.maximum(s, 0.0)
        allowed = (t * tk + kpos) < limit
        sc_ref[t] = jnp.where(allowed, sc, -jnp.inf)
        vmax = jnp.maximum(vmax, jnp.max(jnp.where(allowed, sc, -jnp.inf), axis=-1, keepdims=True))
        vmin = jnp.minimum(vmin, jnp.min(jnp.where(allowed, sc, jnp.inf), axis=-1, keepdims=True))
        return vmin, vmax

    vmin, vmax = lax.fori_loop(0, n_tiles, score_tile,
                               (jnp.full((qb, 1), jnp.inf, F32), jnp.full((qb, 1), -jnp.inf, F32)))

    def count_ge(thr):
        def body(t, acc):
            ind = jnp.where(sc_ref[t] >= thr, 1.0, 0.0)
            for c in range(tk // LANES):
                acc = acc + ind[:, c * LANES:(c + 1) * LANES]
            return acc
        acc = lax.fori_loop(0, n_tiles, body, jnp.zeros((qb, LANES), F32))
        return jnp.sum(acc, axis=-1, keepdims=True)

    def max_below(bound):
        def body(t, acc):
            s = sc_ref[t]
            s = jnp.where(s < bound, s, -jnp.inf)
            for c in range(tk // LANES):
                acc = jnp.maximum(acc, s[:, c * LANES:(c + 1) * LANES])
            return acc
        acc = lax.fori_loop(0, n_tiles, body, jnp.full((qb, LANES), -jnp.inf, F32))
        return jnp.max(acc, axis=-1, keepdims=True)

    def bisect(_, carry):
        lo, hb, hi, c_hi = carry
        mid = 0.5 * lo + 0.5 * hb
        c = count_ge(mid)
        up = c >= k_eff
        return (jnp.where(up, mid, lo), jnp.where(up, hb, mid),
                jnp.where(up, hi, mid), jnp.where(up, c_hi, c))

    lo, hb, hi, c_hi = lax.fori_loop(
        0, DSA_BISECT, bisect,
        (vmin, vmax, jnp.full((qb, 1), jnp.inf, F32), jnp.zeros((qb, 1), F32)))

    max_steps = n_tiles * tk

    def not_done(carry):
        return jnp.logical_and(carry[0] > 0, carry[1] < max_steps)

    def step(carry):
        _, it, hi, c_hi, thr, done = carry
        cand = max_below(hi)
        c = count_ge(cand)
        ok = jnp.logical_and(done < 0.5, c >= k_eff)
        thr = jnp.where(ok, cand, thr)
        done_new = jnp.where(ok, 1.0, done)
        moving = done_new < 0.5
        hi = jnp.where(moving, cand, hi)
        c_hi = jnp.where(moving, c, c_hi)
        remaining = jnp.sum(jnp.where(moving, 1.0, 0.0)).astype(jnp.int32)
        return remaining, it + 1, hi, c_hi, thr, done_new

    _, _, _, c_hi, thr, _ = lax.while_loop(
        not_done, step, (jnp.int32(1), jnp.int32(0), hi, c_hi, vmin, jnp.zeros((qb, 1), F32)))
    n_tie = k_eff - c_hi

    q4 = jnp.concatenate([q_ref[:, h * LANES:(h + 1) * LANES] for h in range(DSA_HEADS)], axis=0)
    ti = lax.broadcasted_iota(jnp.int32, (tk, tk + LANES), 0)
    tj = lax.broadcasted_iota(jnp.int32, (tk, tk + LANES), 1)
    prefix_mat = jnp.where(jnp.logical_or(ti < tj, tj >= tk), 1.0, 0.0).astype(BF16)
    scale = DSA_HEAD_DIM ** -0.5
    neg = -1e30

    def attend(t, carry):
        m, l, acc, seen = carry
        s = sc_ref[t]
        tie = s == thr
        pc = jnp.dot(jnp.where(tie, 1.0, 0.0).astype(BF16), prefix_mat, preferred_element_type=F32)
        sel = jnp.logical_or(s > thr, jnp.logical_and(tie, seen + pc[:, :tk] < n_tie))
        seen = seen + pc[:, tk:tk + 1]
        kt = k_ref[pl.ds(pl.multiple_of(t * tk, tk), tk), :]
        vt = v_ref[pl.ds(pl.multiple_of(t * tk, tk), tk), :]
        lg = lax.dot_general(q4, kt, nt, preferred_element_type=F32) * scale
        lg = lg.reshape(DSA_HEADS, qb, tk)
        sel3 = sel[None]
        lg = jnp.where(sel3, lg, neg)
        m_new = jnp.maximum(m, jnp.max(lg, axis=-1, keepdims=True))
        alpha = jnp.exp(m - m_new)
        p = jnp.where(sel3, jnp.exp(lg - m_new), 0.0)
        l = l * alpha + jnp.sum(p, axis=-1, keepdims=True)
        pv = jnp.dot(p.reshape(DSA_HEADS * qb, tk).astype(BF16), vt, preferred_element_type=F32)
        acc = acc * alpha + pv.reshape(DSA_HEADS, qb, LANES)
        return m_new, l, acc, seen

    m0 = jnp.full((DSA_HEADS, qb, 1), neg, F32)
    l0 = jnp.zeros((DSA_HEADS, qb, 1), F32)
    a0 = jnp.zeros((DSA_HEADS, qb, LANES), F32)
    _, l, acc, _ = lax.fori_loop(0, n_tiles, attend, (m0, l0, a0, jnp.zeros((qb, 1), F32)))
    out = acc / l
    for h in range(DSA_HEADS):
        o_ref[:, h * LANES:(h + 1) * LANES] = out[h].astype(BF16)


def _dsa(iq, misc, dq, ik2, dk, dv, batch, seq):
    qb, tk = DSA_QB, DSA_TK
    assert seq % qb == 0 and seq % tk == 0 and qb % CHUNK == 0
    nq = seq // qb
    n_sel = min(TOPK_MAX, seq // 4)

    def qrow(b, j):
        return (b * nq + j, 0)

    def kv(b, j):
        return (b, 0)

    return pl.pallas_call(
        functools.partial(_dsa_kernel, n_sel=n_sel),
        grid=(batch, nq),
        in_specs=[
            pl.BlockSpec((qb, IDX_Q), qrow),
            pl.BlockSpec((qb, LANES), qrow),
            pl.BlockSpec((qb, DSA_Q), qrow),
            pl.BlockSpec((seq, LANES), kv),
            pl.BlockSpec((seq, DSA_HEAD_DIM), kv),
            pl.BlockSpec((seq, DSA_HEAD_DIM), kv),
        ],
        out_specs=pl.BlockSpec((qb, DSA_Q), qrow),
        out_shape=jax.ShapeDtypeStruct((batch * seq, DSA_Q), BF16),
        scratch_shapes=[pltpu.VMEM((seq // tk, qb, tk), F32)],
        compiler_params=pltpu.CompilerParams(dimension_semantics=("parallel", "arbitrary"),
                                             vmem_limit_bytes=VMEM_LIMIT),
        name="dsa",
    )(iq, misc, dq, ik2, dk, dv)


def _out_ffn_kernel(gla_ref, dsa_ref, x_ref, wo_ref, n1_ref, n2_ref, w1_ref, w2_ref, n3_ref, o_ref):
    mixed = (jnp.dot(gla_ref[...], wo_ref[:GLA_V, :], preferred_element_type=F32)
             + jnp.dot(dsa_ref[...], wo_ref[GLA_V:, :], preferred_element_type=F32))
    x1 = x_ref[...] + _rms(mixed, n1_ref[...])
    h = _rms(x1, n2_ref[...]).astype(BF16)
    d_ff = w1_ref.shape[1]
    acc = jnp.zeros(x1.shape, F32)
    for f in range(d_ff // FFN_TF):
        cols = slice(f * FFN_TF, (f + 1) * FFN_TF)
        a = jnp.maximum(jnp.dot(h, w1_ref[:, cols], preferred_element_type=F32), 0.0)
        acc = acc + jnp.dot((a * a).astype(BF16), w2_ref[cols, :], preferred_element_type=F32)
    o_ref[...] = x1 + _rms(acc, n3_ref[...])


def _out_ffn(gla, dsa, x2, wo, n1, n2, w1, w2, n3):
    t, d = x2.shape
    tm = FFN_TM
    d_ff = w1.shape[1]
    assert t % tm == 0 and d_ff % FFN_TF == 0

    def row(i):
        return (i, 0)

    def const(i):
        return (0, 0)

    def resident(shape):
        return pl.BlockSpec(shape, const, pipeline_mode=pl.Buffered(1))

    return pl.pallas_call(
        _out_ffn_kernel,
        grid=(t // tm,),
        in_specs=[
            pl.BlockSpec((tm, GLA_V), row),
            pl.BlockSpec((tm, DSA_Q), row),
            pl.BlockSpec((tm, d), row),
            resident((GLA_V + DSA_Q, d)),
            resident((1, d)),
            resident((1, d)),
            resident((d, d_ff)),
            resident((d_ff, d)),
            resident((1, d)),
        ],
        out_specs=pl.BlockSpec((tm, d), row),
        out_shape=jax.ShapeDtypeStruct((t, d), F32),
        compiler_params=pltpu.CompilerParams(dimension_semantics=("parallel",), vmem_limit_bytes=VMEM_LIMIT),
        name="out_ffn",
    )(gla, dsa, x2, wo, n1, n2, w1, w2, n3)


def _rope_tables(seq):
    pos = jnp.arange(seq, dtype=F32)[:, None]

    def table(dim, reps):
        inv = ROPE_THETA ** (-jnp.arange(0, dim, 2, dtype=F32) / dim)
        ang = pos * inv[None, :]
        cos, sin = jnp.cos(ang), jnp.sin(ang)
        return (jnp.tile(jnp.concatenate([cos, cos], axis=-1), (1, reps)),
                jnp.tile(jnp.concatenate([-sin, sin], axis=-1), (1, reps)))

    cos128, sin128 = table(DSA_HEAD_DIM, 1)
    cos64, sin64 = table(IDX_DIM, LANES // IDX_DIM)
    return cos128, sin128, cos64, sin64


def _pack_w_in(w_in):
    sizes = (GLA_QK, GLA_QK, GLA_V, GLA_GATE_RANK, GLA_V, DSA_Q, DSA_HEAD_DIM, DSA_HEAD_DIM, IDX_Q, IDX_DIM, IDX_HEADS)
    offs = np.concatenate([[0], np.cumsum(sizes)])
    g_q, g_k, g_v, g_lr, g_r, d_q, d_k, d_v, i_q, i_k, i_w = [w_in[:, offs[i]:offs[i + 1]] for i in range(len(sizes))]
    pad = jnp.zeros((w_in.shape[0], LANES - IDX_DIM - IDX_HEADS - GLA_GATE_RANK), w_in.dtype)
    return jnp.concatenate([g_q, g_k, g_v, g_r, d_q, d_k, d_v, i_q, i_k, i_w, g_lr, pad], axis=1)


def kernel(x, norm_mix_pre, w_in, gla_gate_w2, gla_gate_b, gla_norm_w, idx_k_norm_w, idx_k_norm_b,
           w_out, norm_mix_post, norm_ffn_pre, w_ff1, w_ff2, norm_ffn_post):
    batch, seq, d = x.shape
    tabs = _rope_tables(seq)
    x2 = x.reshape(batch * seq, d)
    for l in range(w_in.shape[0]):
        w_pack = _pack_w_in(w_in[l]).astype(BF16)
        w2_pad = jnp.zeros((LANES, GLA_QK), F32).at[M_LR:M_LR + GLA_GATE_RANK].set(gla_gate_w2[l]).astype(BF16)
        lnw = jnp.zeros((1, LANES), F32).at[0, :IDX_DIM].set(idx_k_norm_w[l])
        lnb = jnp.zeros((1, LANES), F32).at[0, :IDX_DIM].set(idx_k_norm_b[l])
        (gq, gk, gv, gr, glog, dq, dk, dv, iq, ik2, misc) = _projection(
            x2, norm_mix_pre[l][None], w_pack, w2_pad, gla_gate_b[l][None], lnw, lnb, tabs, seq)
        gla = _gla(gq, gk, glog, gv, gr, gla_norm_w[l][None], batch, seq)
        dsa = _dsa(iq, misc, dq, ik2, dk, dv, batch, seq)
        x2 = _out_ffn(gla, dsa, x2, w_out[l].astype(BF16), norm_mix_post[l][None], norm_ffn_pre[l][None],
                      w_ff1[l].astype(BF16), w_ff2[l].astype(BF16), norm_ffn_post[l][None])
    return x2.reshape(batch, seq, d)
```

```python
import functools

import jax
import jax.numpy as jnp
import numpy as np
from jax import lax
from jax.experimental import pallas as pl
from jax.experimental.pallas import tpu as pltpu

F32 = jnp.float32
BF16 = jnp.bfloat16

EPS = 1e-6
ROPE_THETA = 10000.0
CHUNK = 64
GLA_HEADS = 4
GLA_DK = 64
GLA_DV = 128
GLA_GATE_RANK = 16
GLA_TAU = 16.0
DSA_HEADS = 4
DSA_HEAD_DIM = 128
IDX_HEADS = 4
IDX_DIM = 64
TOPK_MAX = 256

GLA_QK = GLA_HEADS * GLA_DK
GLA_V = GLA_HEADS * GLA_DV
DSA_Q = DSA_HEADS * DSA_HEAD_DIM
IDX_Q = IDX_HEADS * IDX_DIM

LANES = 128
VMEM_LIMIT = 56 * 1024 * 1024

C_GQ = 0
C_GK = C_GQ + GLA_QK
C_GV = C_GK + GLA_QK
C_GR = C_GV + GLA_V
C_DQ = C_GR + GLA_V
C_DK = C_DQ + DSA_Q
C_DV = C_DK + DSA_HEAD_DIM
C_IQ = C_DV + DSA_HEAD_DIM
C_MISC = C_IQ + IDX_Q
D_PACK = C_MISC + LANES
M_IW = IDX_DIM
M_LR = IDX_DIM + IDX_HEADS

PROJ_TM = 512
GLA_TS = 512
DSA_QB = 256
DSA_TK = 256
DSA_BISECT = 14
FFN_TM = 512
FFN_TF = 1024


def _rms(x, w):
    return x * lax.rsqrt(jnp.mean(x * x, axis=-1, keepdims=True) + EPS) * w


def _proj_kernel(x_ref, nw_ref, w_ref, w2_ref, gb_ref, lnw_ref, lnb_ref,
                 cos128_ref, sin128_ref, cos64_ref, sin64_ref,
                 gq_ref, gk_ref, gv_ref, gr_ref, glog_ref,
                 dq_ref, dk_ref, dv_ref, iq_ref, ik_ref, misc_ref):
    hb = _rms(x_ref[...], nw_ref[...]).astype(BF16)

    def proj(lo, width):
        return jnp.dot(hb, w_ref[:, lo:lo + width], preferred_element_type=F32)

    gq_ref[...] = proj(C_GQ, GLA_QK)
    gk_ref[...] = proj(C_GK, GLA_QK)
    gv_ref[...] = proj(C_GV, GLA_V).astype(BF16)
    gr_ref[...] = proj(C_GR, GLA_V)
    dv = proj(C_DV, DSA_HEAD_DIM)
    for c in range(dv_ref.shape[0]):
        dv_ref[c] = dv[c * DSA_TK:(c + 1) * DSA_TK, :].T.astype(BF16)

    cos128 = cos128_ref[...]
    sin128 = sin128_ref[...]
    cos64 = cos64_ref[...]
    sin64 = sin64_ref[...]
    lane = lax.broadcasted_iota(jnp.int32, (1, LANES), 1)
    first_half64 = (lane % IDX_DIM) < (IDX_DIM // 2)

    def rope128(t):
        return t * cos128 + pltpu.roll(t, LANES // 2, 1) * sin128

    def rope64(t):
        rot = jnp.where(first_half64, pltpu.roll(t, LANES - IDX_DIM // 2, 1), pltpu.roll(t, IDX_DIM // 2, 1))
        return t * cos64 + rot * sin64

    dq = proj(C_DQ, DSA_Q)
    for h in range(DSA_HEADS):
        sl = slice(h * LANES, (h + 1) * LANES)
        dq_ref[:, sl] = rope128(dq[:, sl]).astype(BF16)
    dk_ref[...] = rope128(proj(C_DK, DSA_HEAD_DIM)).astype(BF16)

    iq = proj(C_IQ, IDX_Q)
    for c in range(IDX_Q // LANES):
        sl = slice(c * LANES, (c + 1) * LANES)
        iq_ref[:, sl] = rope64(iq[:, sl]).astype(BF16)

    misc = proj(C_MISC, LANES)
    misc_ref[...] = misc * (IDX_HEADS ** -0.5 * IDX_DIM ** -0.5)

    is_ik = lane < IDX_DIM
    mu = jnp.sum(jnp.where(is_ik, misc, 0.0), axis=-1, keepdims=True) * (1.0 / IDX_DIM)
    d = jnp.where(is_ik, misc - mu, 0.0)
    var = jnp.sum(d * d, axis=-1, keepdims=True) * (1.0 / IDX_DIM)
    y = d * lax.rsqrt(var + EPS) * lnw_ref[...] + lnb_ref[...]
    yr = rope64(y)
    ik_ref[...] = (yr + pltpu.roll(yr, LANES // 2, 1)).astype(BF16)

    z = jnp.dot(misc.astype(BF16), w2_ref[...], preferred_element_type=F32) + gb_ref[...]
    glog_ref[...] = (jnp.minimum(z, 0.0) - jnp.log1p(jnp.exp(-jnp.abs(z)))) * (1.0 / GLA_TAU)


def _projection(x2, nw, w_pack, w2_pad, gb, lnw, lnb, tabs, seq):
    t, d = x2.shape
    tm = PROJ_TM
    assert t % tm == 0 and seq % tm == 0
    tiles_per_seq = seq // tm

    def row(i):
        return (i, 0)

    def const(i):
        return (0, 0)

    def pos(i):
        return (i % tiles_per_seq, 0)

    widths = [(GLA_QK, F32), (GLA_QK, F32), (GLA_V, BF16), (GLA_V, F32), (GLA_QK, F32),
              (DSA_Q, BF16), (DSA_HEAD_DIM, BF16), (DSA_HEAD_DIM, BF16), (IDX_Q, BF16),
              (LANES, BF16), (LANES, F32)]
    DV_SLOT = 7
    assert tm % DSA_TK == 0
    dvt_spec = pl.BlockSpec((tm // DSA_TK, DSA_HEAD_DIM, DSA_TK), lambda i: (i, 0, 0))
    dvt_shape = jax.ShapeDtypeStruct((t // DSA_TK, DSA_HEAD_DIM, DSA_TK), BF16)
    return pl.pallas_call(
        _proj_kernel,
        grid=(t // tm,),
        in_specs=[
            pl.BlockSpec((tm, d), row),
            pl.BlockSpec((1, d), const),
            pl.BlockSpec((d, D_PACK), const),
            pl.BlockSpec((LANES, GLA_QK), const),
            pl.BlockSpec((1, GLA_QK), const),
            pl.BlockSpec((1, LANES), const),
            pl.BlockSpec((1, LANES), const),
            pl.BlockSpec((tm, LANES), pos),
            pl.BlockSpec((tm, LANES), pos),
            pl.BlockSpec((tm, LANES), pos),
            pl.BlockSpec((tm, LANES), pos),
        ],
        out_specs=[dvt_spec if n == DV_SLOT else pl.BlockSpec((tm, w), row) for n, (w, _) in enumerate(widths)],
        out_shape=[dvt_shape if n == DV_SLOT else jax.ShapeDtypeStruct((t, w), dt)
                   for n, (w, dt) in enumerate(widths)],
        compiler_params=pltpu.CompilerParams(dimension_semantics=("parallel",), vmem_limit_bytes=VMEM_LIMIT),
        name="proj",
    )(x2, nw, w_pack, w2_pad, gb, lnw, lnb, *tabs)


def _gla_kernel(q_ref, k_ref, g_ref, v_ref, r_ref, nw_ref, o_ref, st_ref):
    @pl.when(pl.program_id(1) == 0)
    def _():
        st_ref[...] = jnp.zeros_like(st_ref)

    ts = q_ref.shape[0]
    n_pairs = GLA_HEADS // 2
    ri = lax.broadcasted_iota(jnp.int32, (CHUNK, CHUNK), 0)
    ci = lax.broadcasted_iota(jnp.int32, (CHUNK, CHUNK), 1)
    causal = ri >= ci
    tri = jnp.where(causal, 1.0, 0.0).astype(BF16)
    lane = lax.broadcasted_iota(jnp.int32, (1, LANES), 1)
    half_mask = [lane < GLA_DK, lane >= GLA_DK]
    nw = nw_ref[...]
    nt = (((1,), (1,)), ((), ()))
    tn = (((0,), (0,)), ((), ()))

    for c in range(ts // CHUNK):
        rows = slice(c * CHUNK, (c + 1) * CHUNK)
        g = g_ref[rows, :]
        g1 = g.astype(BF16)
        g2 = (g - g1.astype(F32)).astype(BF16)
        g3 = (g - g1.astype(F32) - g2.astype(F32)).astype(BF16)
        bcum = (jnp.dot(tri, g1, preferred_element_type=F32)
                + jnp.dot(tri, g2, preferred_element_type=F32)
                + jnp.dot(tri, g3, preferred_element_type=F32))
        b_last = bcum[CHUNK - 1:CHUNK, :]
        q_dec = q_ref[rows, :] * (GLA_DK ** -0.5) * jnp.exp(bcum)
        k = k_ref[rows, :]
        k_inv = (k * jnp.exp(-bcum)).astype(BF16)
        k_end = k * jnp.exp(b_last - bcum)
        decay = jnp.exp(b_last)
        for p in range(n_pairs):
            lanes = slice(p * LANES, (p + 1) * LANES)
            qd = q_dec[:, lanes]
            ki = k_inv[:, lanes]
            ke = k_end[:, lanes]
            st = st_ref[p]
            st_b = st.astype(BF16)
            ke_parts = []
            v_parts = []
            for e in range(2):
                h = 2 * p + e
                vh = v_ref[rows, h * GLA_DV:(h + 1) * GLA_DV]
                qm = jnp.where(half_mask[e], qd, 0.0).astype(BF16)
                attn = lax.dot_general(qm, ki, nt, preferred_element_type=F32)
                attn = jnp.where(causal, attn, 0.0).astype(BF16)
                o = (jnp.dot(attn, vh, preferred_element_type=F32)
                     + lax.dot_general(qm, st_b, nt, preferred_element_type=F32))
                o = _rms(o, nw)
                r = r_ref[rows, h * GLA_DV:(h + 1) * GLA_DV]
                o_ref[rows, h * GLA_DV:(h + 1) * GLA_DV] = (o * (r * jax.nn.sigmoid(r))).astype(BF16)
                ke_parts.append(jnp.where(half_mask[e], ke, 0.0).astype(BF16))
                v_parts.append(vh)
            upd = lax.dot_general(jnp.concatenate(v_parts, axis=0), jnp.concatenate(ke_parts, axis=0), tn,
                                  preferred_element_type=F32)
            st_ref[p] = st * decay[:, lanes] + upd


def _gla(gq, gk, glog, gv, gr, nw, batch, seq):
    ts = GLA_TS
    assert seq % ts == 0 and ts % CHUNK == 0
    n = seq // ts

    def row(b, s):
        return (b * n + s, 0)

    return pl.pallas_call(
        _gla_kernel,
        grid=(batch, n),
        in_specs=[
            pl.BlockSpec((ts, GLA_QK), row),
            pl.BlockSpec((ts, GLA_QK), row),
            pl.BlockSpec((ts, GLA_QK), row),
            pl.BlockSpec((ts, GLA_V), row),
            pl.BlockSpec((ts, GLA_V), row),
            pl.BlockSpec((1, GLA_DV), lambda b, s: (0, 0)),
        ],
        out_specs=pl.BlockSpec((ts, GLA_V), row),
        out_shape=jax.ShapeDtypeStruct((batch * seq, GLA_V), BF16),
        scratch_shapes=[pltpu.VMEM((GLA_HEADS // 2, GLA_DV, 2 * GLA_DK), F32)],
        compiler_params=pltpu.CompilerParams(dimension_semantics=("parallel", "arbitrary"),
                                             vmem_limit_bytes=VMEM_LIMIT),
        name="gla",
    )(gq, gk, glog, gv, gr, nw)


def _dsa_kernel(qi_ref, misc_ref, q_ref, ik_ref, k_ref, vt_ref, o_ref, sc_ref, acc_ref, *, n_sel):
    qb = qi_ref.shape[0]
    tk = sc_ref.shape[1]
    j = pl.program_id(1)
    n_tiles = ((j + 1) * qb + tk - 1) // tk
    nt = (((1,), (1,)), ((), ()))
    groups = tk // 8

    col = lax.broadcasted_iota(jnp.int32, (1, qb), 1)
    limit = j * qb + (col // CHUNK + 1) * CHUNK
    k_eff = jnp.minimum(limit, n_sel).astype(F32)
    key_in_tile = lax.broadcasted_iota(jnp.int32, (tk, qb), 0)
    lane = lax.broadcasted_iota(jnp.int32, (1, LANES), 1)

    misc_t = misc_ref[...].T
    qi = qi_ref[...]
    qm = []
    wh = []
    for h in range(IDX_HEADS):
        pair = qi[:, (h // 2) * LANES:(h // 2 + 1) * LANES].astype(F32)
        in_head = (lane // IDX_DIM) == (h % 2)
        qm.append(jnp.where(in_head, pair, 0.0).astype(BF16))
        wh.append(misc_t[M_IW + h:M_IW + h + 1, :])
    qm4 = jnp.concatenate(qm, axis=0)

    def fold_max(a):
        return jnp.max(a.reshape(groups, 8, a.shape[-1]), axis=0)

    def fold_min(a):
        return jnp.min(a.reshape(groups, 8, a.shape[-1]), axis=0)

    def fold_sum(a):
        return jnp.sum(a.reshape(groups, 8, a.shape[-1]), axis=0)

    def score_tile(t, carry):
        pmin, pmax = carry
        kt = ik_ref[pl.ds(pl.multiple_of(t * tk, tk), tk), :]
        s4 = lax.dot_general(kt, qm4, nt, preferred_element_type=F32)
        sc = wh[0] * jnp.maximum(s4[:, :qb], 0.0)
        for h in range(1, IDX_HEADS):
            sc = sc + wh[h] * jnp.maximum(s4[:, h * qb:(h + 1) * qb], 0.0)
        scm = jnp.where(key_in_tile < limit - t * tk, sc, -jnp.inf)
        sc_ref[t] = scm
        return jnp.minimum(pmin, fold_min(sc)), jnp.maximum(pmax, fold_max(scm))

    pmin, pmax = lax.fori_loop(0, n_tiles, score_tile,
                               (jnp.full((8, qb), jnp.inf, F32), jnp.full((8, qb), -jnp.inf, F32)))
    vmin = jnp.min(pmin, axis=0, keepdims=True)
    vmax = jnp.max(pmax, axis=0, keepdims=True)

    def count_ge(thr):
        def body(t, acc):
            return acc + fold_sum(jnp.where(sc_ref[t] >= thr, 1.0, 0.0))
        acc = lax.fori_loop(0, n_tiles, body, jnp.zeros((8, qb), F32))
        return jnp.sum(acc, axis=0, keepdims=True)

    def max_below(bound):
        def body(t, acc):
            s = sc_ref[t]
            return jnp.maximum(acc, fold_max(jnp.where(s < bound, s, -jnp.inf)))
        acc = lax.fori_loop(0, n_tiles, body, jnp.full((8, qb), -jnp.inf, F32))
        return jnp.max(acc, axis=0, keepdims=True)

    def bisect(_, carry):
        lo, hb, hi, c_hi = carry
        mid = 0.5 * lo + 0.5 * hb
        c = count_ge(mid)
        up = c >= k_eff
        return (jnp.where(up, mid, lo), jnp.where(up, hb, mid),
                jnp.where(up, hi, mid), jnp.where(up, c_hi, c))

    lo, hb, hi, c_hi = lax.fori_loop(
        0, DSA_BISECT, bisect,
        (vmin, vmax, jnp.full((1, qb), jnp.inf, F32), jnp.zeros((1, qb), F32)))

    max_steps = n_tiles * tk

    def not_done(carry):
        return jnp.logical_and(carry[0] > 0, carry[1] < max_steps)

    def step(carry):
        _, it, hi, c_hi, thr, done = carry
        cand = max_below(hi)
        c = count_ge(cand)
        ok = jnp.logical_and(done < 0.5, c >= k_eff)
        thr = jnp.where(ok, cand, thr)
        done_new = jnp.where(ok, 1.0, done)
        moving = done_new < 0.5
        hi = jnp.where(moving, cand, hi)
        c_hi = jnp.where(moving, c, c_hi)
        remaining = jnp.sum(jnp.where(moving, 1.0, 0.0)).astype(jnp.int32)
        return remaining, it + 1, hi, c_hi, thr, done_new

    _, _, _, c_hi, thr, _ = lax.while_loop(
        not_done, step, (jnp.int32(1), jnp.int32(0), hi, c_hi, vmin, jnp.zeros((1, qb), F32)))
    n_tie = k_eff - c_hi

    q4 = jnp.concatenate([q_ref[:, h * LANES:(h + 1) * LANES] for h in range(DSA_HEADS)], axis=0)
    ti = lax.broadcasted_iota(jnp.int32, (tk, tk), 0)
    tj = lax.broadcasted_iota(jnp.int32, (tk, tk), 1)
    tri_incl = jnp.where(tj <= ti, 1.0, 0.0).astype(BF16)
    scale = DSA_HEAD_DIM ** -0.5
    neg = -1e30
    acc_ref[...] = jnp.zeros_like(acc_ref)

    def attend(t, carry):
        m, l, seen = carry
        s = sc_ref[t]
        tie = s == thr
        tie_f = jnp.where(tie, 1.0, 0.0)
        incl = jnp.dot(tri_incl, tie_f.astype(BF16), preferred_element_type=F32)
        sel = jnp.logical_or(s > thr, jnp.logical_and(tie, seen + incl - tie_f < n_tie))
        seen = seen + incl[tk - 1:tk, :]
        kt = k_ref[pl.ds(pl.multiple_of(t * tk, tk), tk), :]
        lg = lax.dot_general(kt, q4, nt, preferred_element_type=F32)
        m_out, l_out, p_parts, a_parts = [], [], [], []
        for h in range(DSA_HEADS):
            lanes = slice(h * qb, (h + 1) * qb)
            lgh = jnp.where(sel, lg[:, lanes] * scale, neg)
            m_new = jnp.maximum(m[h], jnp.max(fold_max(lgh), axis=0, keepdims=True))
            alpha = jnp.exp(m[h] - m_new)
            p = jnp.exp(lgh - m_new)
            l_out.append(l[h] * alpha + jnp.sum(fold_sum(p), axis=0, keepdims=True))
            m_out.append(m_new)
            p_parts.append(p.astype(BF16))
            a_parts.append(alpha)
        pv = jnp.dot(vt_ref[t], jnp.concatenate(p_parts, axis=1), preferred_element_type=F32)
        acc_ref[...] = acc_ref[...] * jnp.concatenate(a_parts, axis=1) + pv
        return tuple(m_out), tuple(l_out), seen

    m0 = tuple(jnp.full((1, qb), neg, F32) for _ in range(DSA_HEADS))
    l0 = tuple(jnp.zeros((1, qb), F32) for _ in range(DSA_HEADS))
    _, l, _ = lax.fori_loop(0, n_tiles, attend, (m0, l0, jnp.zeros((1, qb), F32)))
    for h in range(DSA_HEADS):
        out_t = acc_ref[:, h * qb:(h + 1) * qb] / l[h]
        o_ref[:, h * LANES:(h + 1) * LANES] = out_t.T.astype(BF16)


def _dsa(iq, misc, dq, ik2, dk, dvt, batch, seq):
    qb, tk = DSA_QB, DSA_TK
    assert seq % qb == 0 and seq % tk == 0 and qb % CHUNK == 0
    nq = seq // qb
    nkt = seq // tk
    n_sel = min(TOPK_MAX, seq // 4)

    def qrow(b, j):
        return (b * nq + j, 0)

    def kv(b, j):
        return (b, 0)

    return pl.pallas_call(
        functools.partial(_dsa_kernel, n_sel=n_sel),
        grid=(batch, nq),
        in_specs=[
            pl.BlockSpec((qb, IDX_Q), qrow),
            pl.BlockSpec((qb, LANES), qrow),
            pl.BlockSpec((qb, DSA_Q), qrow),
            pl.BlockSpec((seq, LANES), kv),
            pl.BlockSpec((seq, DSA_HEAD_DIM), kv),
            pl.BlockSpec((nkt, DSA_HEAD_DIM, tk), lambda b, j: (b, 0, 0)),
        ],
        out_specs=pl.BlockSpec((qb, DSA_Q), qrow),
        out_shape=jax.ShapeDtypeStruct((batch * seq, DSA_Q), BF16),
        scratch_shapes=[pltpu.VMEM((nkt, tk, qb), F32),
                        pltpu.VMEM((DSA_HEAD_DIM, DSA_HEADS * qb), F32)],
        compiler_params=pltpu.CompilerParams(dimension_semantics=("parallel", "arbitrary"),
                                             vmem_limit_bytes=VMEM_LIMIT),
        name="dsa",
    )(iq, misc, dq, ik2, dk, dvt)


def _out_ffn_kernel(gla_ref, dsa_ref, x_ref, wo_ref, n1_ref, n2_ref, w1_ref, w2_ref, n3_ref, o_ref):
    mixed = (jnp.dot(gla_ref[...], wo_ref[:GLA_V, :], preferred_element_type=F32)
             + jnp.dot(dsa_ref[...], wo_ref[GLA_V:, :], preferred_element_type=F32))
    x1 = x_ref[...] + _rms(mixed, n1_ref[...])
    h = _rms(x1, n2_ref[...]).astype(BF16)
    d_ff = w1_ref.shape[1]
    acc = jnp.zeros(x1.shape, F32)
    for f in range(d_ff // FFN_TF):
        cols = slice(f * FFN_TF, (f + 1) * FFN_TF)
        a = jnp.maximum(jnp.dot(h, w1_ref[:, cols], preferred_element_type=F32), 0.0)
        acc = acc + jnp.dot((a * a).astype(BF16), w2_ref[cols, :], preferred_element_type=F32)
    o_ref[...] = x1 + _rms(acc, n3_ref[...])


def _out_ffn(gla, dsa, x2, wo, n1, n2, w1, w2, n3):
    t, d = x2.shape
    tm = FFN_TM
    d_ff = w1.shape[1]
    assert t % tm == 0 and d_ff % FFN_TF == 0

    def row(i):
        return (i, 0)

    def const(i):
        return (0, 0)

    def resident(shape):
        return pl.BlockSpec(shape, const, pipeline_mode=pl.Buffered(1))

    return pl.pallas_call(
        _out_ffn_kernel,
        grid=(t // tm,),
        in_specs=[
            pl.BlockSpec((tm, GLA_V), row),
            pl.BlockSpec((tm, DSA_Q), row),
            pl.BlockSpec((tm, d), row),
            resident((GLA_V + DSA_Q, d)),
            resident((1, d)),
            resident((1, d)),
            resident((d, d_ff)),
            resident((d_ff, d)),
            resident((1, d)),
        ],
        out_specs=pl.BlockSpec((tm, d), row),
        out_shape=jax.ShapeDtypeStruct((t, d), F32),
        compiler_params=pltpu.CompilerParams(dimension_semantics=("parallel",), vmem_limit_bytes=VMEM_LIMIT),
        name="out_ffn",
    )(gla, dsa, x2, wo, n1, n2, w1, w2, n3)


def _rope_tables(seq):
    pos = jnp.arange(seq, dtype=F32)[:, None]

    def table(dim, reps):
        inv = ROPE_THETA ** (-jnp.arange(0, dim, 2, dtype=F32) / dim)
        ang = pos * inv[None, :]
        cos, sin = jnp.cos(ang), jnp.sin(ang)
        return (jnp.tile(jnp.concatenate([cos, cos], axis=-1), (1, reps)),
                jnp.tile(jnp.concatenate([-sin, sin], axis=-1), (1, reps)))

    cos128, sin128 = table(DSA_HEAD_DIM, 1)
    cos64, sin64 = table(IDX_DIM, LANES // IDX_DIM)
    return cos128, sin128, cos64, sin64


def _pack_w_in(w_in):
    sizes = (GLA_QK, GLA_QK, GLA_V, GLA_GATE_RANK, GLA_V, DSA_Q, DSA_HEAD_DIM, DSA_HEAD_DIM, IDX_Q, IDX_DIM, IDX_HEADS)
    offs = np.concatenate([[0], np.cumsum(sizes)])
    g_q, g_k, g_v, g_lr, g_r, d_q, d_k, d_v, i_q, i_k, i_w = [w_in[:, offs[i]:offs[i + 1]] for i in range(len(sizes))]
    pad = jnp.zeros((w_in.shape[0], LANES - IDX_DIM - IDX_HEADS - GLA_GATE_RANK), w_in.dtype)
    return jnp.concatenate([g_q, g_k, g_v, g_r, d_q, d_k, d_v, i_q, i_k, i_w, g_lr, pad], axis=1)


def kernel(x, norm_mix_pre, w_in, gla_gate_w2, gla_gate_b, gla_norm_w, idx_k_norm_w, idx_k_norm_b,
           w_out, norm_mix_post, norm_ffn_pre, w_ff1, w_ff2, norm_ffn_post):
    batch, seq, d = x.shape
    tabs = _rope_tables(seq)
    x2 = x.reshape(batch * seq, d)
    for l in range(w_in.shape[0]):
        w_pack = _pack_w_in(w_in[l]).astype(BF16)
        w2_pad = jnp.zeros((LANES, GLA_QK), F32).at[M_LR:M_LR + GLA_GATE_RANK].set(gla_gate_w2[l]).astype(BF16)
        lnw = jnp.zeros((1, LANES), F32).at[0, :IDX_DIM].set(idx_k_norm_w[l])
        lnb = jnp.zeros((1, LANES), F32).at[0, :IDX_DIM].set(idx_k_norm_b[l])
        (gq, gk, gv, gr, glog, dq, dk, dvt, iq, ik2, misc) = _projection(
            x2, norm_mix_pre[l][None], w_pack, w2_pad, gla_gate_b[l][None], lnw, lnb, tabs, seq)
        gla = _gla(gq, gk, glog, gv, gr, gla_norm_w[l][None], batch, seq)
        dsa = _dsa(iq, misc, dq, ik2, dk, dvt, batch, seq)
        x2 = _out_ffn(gla, dsa, x2, w_out[l].astype(BF16), norm_mix_post[l][None], norm_ffn_pre[l][None],
                      w_ff1[l].astype(BF16), w_ff2[l].astype(BF16), norm_ffn_post[l][None])
    return x2.reshape(batch, seq, d)
```

```python
import functools

import jax
import jax.numpy as jnp
import numpy as np
from jax import lax
from jax.experimental import pallas as pl
from jax.experimental.pallas import tpu as pltpu

F32 = jnp.float32
BF16 = jnp.bfloat16

EPS = 1e-6
ROPE_THETA = 10000.0
CHUNK = 64
GLA_HEADS = 4
GLA_DK = 64
GLA_DV = 128
GLA_GATE_RANK = 16
GLA_TAU = 16.0
DSA_HEADS = 4
DSA_HEAD_DIM = 128
IDX_HEADS = 4
IDX_DIM = 64
TOPK_MAX = 256

GLA_QK = GLA_HEADS * GLA_DK
GLA_V = GLA_HEADS * GLA_DV
DSA_Q = DSA_HEADS * DSA_HEAD_DIM
IDX_Q = IDX_HEADS * IDX_DIM

LANES = 128
VMEM_LIMIT = 56 * 1024 * 1024

C_GQ = 0
C_GK = C_GQ + GLA_QK
C_GV = C_GK + GLA_QK
C_GR = C_GV + GLA_V
C_DQ = C_GR + GLA_V
C_DK = C_DQ + DSA_Q
C_DV = C_DK + DSA_HEAD_DIM
C_IQ = C_DV + DSA_HEAD_DIM
C_MISC = C_IQ + IDX_Q
D_PACK = C_MISC + LANES
M_IW = IDX_DIM
M_LR = IDX_DIM + IDX_HEADS

PROJ_TM = 512
GLA_TS = 512
DSA_QB = 256
DSA_TK = 256
DSA_BISECT = 16
DSA_ONES = 16
DSA_VROWS = DSA_HEAD_DIM + DSA_ONES
DSA_Q_SCALE = DSA_HEAD_DIM ** -0.5 * 1.4426950408889634
FFN_TM = 512
FFN_TF = 1024


def _rms(x, w):
    return x * lax.rsqrt(jnp.mean(x * x, axis=-1, keepdims=True) + EPS) * w


def _proj_kernel(x_ref, nw_ref, w_ref, w2_ref, gb_ref, lnw_ref, lnb_ref,
                 cos128_ref, sin128_ref, cos64_ref, sin64_ref,
                 gq_ref, gk_ref, gv_ref, gr_ref, glog_ref,
                 dq_ref, dk_ref, dv_ref, iq_ref, ik_ref, misc_ref):
    hb = _rms(x_ref[...], nw_ref[...]).astype(BF16)

    def proj(lo, width):
        return jnp.dot(hb, w_ref[:, lo:lo + width], preferred_element_type=F32)

    gq_ref[...] = proj(C_GQ, GLA_QK)
    gk_ref[...] = proj(C_GK, GLA_QK)
    gv_ref[...] = proj(C_GV, GLA_V).astype(BF16)
    gr_ref[...] = proj(C_GR, GLA_V)
    dv = proj(C_DV, DSA_HEAD_DIM)
    ones = jnp.ones((DSA_ONES, DSA_TK), BF16)
    for c in range(dv_ref.shape[0]):
        dv_t = dv[c * DSA_TK:(c + 1) * DSA_TK, :].T.astype(BF16)
        dv_ref[c] = jnp.concatenate([dv_t, ones], axis=0)

    cos128 = cos128_ref[...]
    sin128 = sin128_ref[...]
    cos64 = cos64_ref[...]
    sin64 = sin64_ref[...]
    lane = lax.broadcasted_iota(jnp.int32, (1, LANES), 1)
    first_half64 = (lane % IDX_DIM) < (IDX_DIM // 2)

    def rope128(t):
        return t * cos128 + pltpu.roll(t, LANES // 2, 1) * sin128

    def rope64(t):
        rot = jnp.where(first_half64, pltpu.roll(t, LANES - IDX_DIM // 2, 1), pltpu.roll(t, IDX_DIM // 2, 1))
        return t * cos64 + rot * sin64

    dq = proj(C_DQ, DSA_Q)
    for h in range(DSA_HEADS):
        sl = slice(h * LANES, (h + 1) * LANES)
        dq_ref[:, sl] = (rope128(dq[:, sl]) * DSA_Q_SCALE).astype(BF16)
    dk_ref[...] = rope128(proj(C_DK, DSA_HEAD_DIM)).astype(BF16)

    iq = proj(C_IQ, IDX_Q)
    for c in range(IDX_Q // LANES):
        sl = slice(c * LANES, (c + 1) * LANES)
        iq_ref[:, sl] = rope64(iq[:, sl]).astype(BF16)

    misc = proj(C_MISC, LANES)
    misc_ref[...] = misc * (IDX_HEADS ** -0.5 * IDX_DIM ** -0.5)

    is_ik = lane < IDX_DIM
    mu = jnp.sum(jnp.where(is_ik, misc, 0.0), axis=-1, keepdims=True) * (1.0 / IDX_DIM)
    d = jnp.where(is_ik, misc - mu, 0.0)
    var = jnp.sum(d * d, axis=-1, keepdims=True) * (1.0 / IDX_DIM)
    y = d * lax.rsqrt(var + EPS) * lnw_ref[...] + lnb_ref[...]
    yr = rope64(y)
    ik_ref[...] = (yr + pltpu.roll(yr, LANES // 2, 1)).astype(BF16)

    z = jnp.dot(misc.astype(BF16), w2_ref[...], preferred_element_type=F32) + gb_ref[...]
    glog_ref[...] = (jnp.minimum(z, 0.0) - jnp.log1p(jnp.exp(-jnp.abs(z)))) * (1.0 / GLA_TAU)


def _projection(x2, nw, w_pack, w2_pad, gb, lnw, lnb, tabs, seq):
    t, d = x2.shape
    tm = PROJ_TM
    assert t % tm == 0 and seq % tm == 0
    tiles_per_seq = seq // tm

    def row(i):
        return (i, 0)

    def const(i):
        return (0, 0)

    def pos(i):
        return (i % tiles_per_seq, 0)

    widths = [(GLA_QK, F32), (GLA_QK, F32), (GLA_V, BF16), (GLA_V, F32), (GLA_QK, F32),
              (DSA_Q, BF16), (DSA_HEAD_DIM, BF16), (DSA_HEAD_DIM, BF16), (IDX_Q, BF16),
              (LANES, BF16), (LANES, F32)]
    DV_SLOT = 7
    assert tm % DSA_TK == 0
    dvt_spec = pl.BlockSpec((tm // DSA_TK, DSA_VROWS, DSA_TK), lambda i: (i, 0, 0))
    dvt_shape = jax.ShapeDtypeStruct((t // DSA_TK, DSA_VROWS, DSA_TK), BF16)
    return pl.pallas_call(
        _proj_kernel,
        grid=(t // tm,),
        in_specs=[
            pl.BlockSpec((tm, d), row),
            pl.BlockSpec((1, d), const),
            pl.BlockSpec((d, D_PACK), const),
            pl.BlockSpec((LANES, GLA_QK), const),
            pl.BlockSpec((1, GLA_QK), const),
            pl.BlockSpec((1, LANES), const),
            pl.BlockSpec((1, LANES), const),
            pl.BlockSpec((tm, LANES), pos),
            pl.BlockSpec((tm, LANES), pos),
            pl.BlockSpec((tm, LANES), pos),
            pl.BlockSpec((tm, LANES), pos),
        ],
        out_specs=[dvt_spec if n == DV_SLOT else pl.BlockSpec((tm, w), row) for n, (w, _) in enumerate(widths)],
        out_shape=[dvt_shape if n == DV_SLOT else jax.ShapeDtypeStruct((t, w), dt)
                   for n, (w, dt) in enumerate(widths)],
        compiler_params=pltpu.CompilerParams(dimension_semantics=("parallel",), vmem_limit_bytes=VMEM_LIMIT),
        name="proj",
    )(x2, nw, w_pack, w2_pad, gb, lnw, lnb, *tabs)


def _gla_kernel(q_ref, k_ref, g_ref, v_ref, r_ref, nw_ref, o_ref, st_ref):
    @pl.when(pl.program_id(1) == 0)
    def _():
        st_ref[...] = jnp.zeros_like(st_ref)

    ts = q_ref.shape[0]
    n_pairs = GLA_HEADS // 2
    ri = lax.broadcasted_iota(jnp.int32, (CHUNK, CHUNK), 0)
    ci = lax.broadcasted_iota(jnp.int32, (CHUNK, CHUNK), 1)
    causal = ri >= ci
    tri = jnp.where(causal, 1.0, 0.0).astype(BF16)
    lane = lax.broadcasted_iota(jnp.int32, (1, LANES), 1)
    half_mask = [lane < GLA_DK, lane >= GLA_DK]
    nw = nw_ref[...]
    nt = (((1,), (1,)), ((), ()))
    tn = (((0,), (0,)), ((), ()))

    for c in range(ts // CHUNK):
        rows = slice(c * CHUNK, (c + 1) * CHUNK)
        g = g_ref[rows, :]
        g1 = g.astype(BF16)
        g2 = (g - g1.astype(F32)).astype(BF16)
        g3 = (g - g1.astype(F32) - g2.astype(F32)).astype(BF16)
        bcum = (jnp.dot(tri, g1, preferred_element_type=F32)
                + jnp.dot(tri, g2, preferred_element_type=F32)
                + jnp.dot(tri, g3, preferred_element_type=F32))
        b_last = bcum[CHUNK - 1:CHUNK, :]
        q_dec = q_ref[rows, :] * (GLA_DK ** -0.5) * jnp.exp(bcum)
        k = k_ref[rows, :]
        k_inv = (k * jnp.exp(-bcum)).astype(BF16)
        k_end = k * jnp.exp(b_last - bcum)
        decay = jnp.exp(b_last)
        for p in range(n_pairs):
            lanes = slice(p * LANES, (p + 1) * LANES)
            qd = q_dec[:, lanes]
            ki = k_inv[:, lanes]
            ke = k_end[:, lanes]
            st = st_ref[p]
            st_b = st.astype(BF16)
            ke_parts = []
            v_parts = []
            for e in range(2):
                h = 2 * p + e
                vh = v_ref[rows, h * GLA_DV:(h + 1) * GLA_DV]
                qm = jnp.where(half_mask[e], qd, 0.0).astype(BF16)
                attn = lax.dot_general(qm, ki, nt, preferred_element_type=F32)
                attn = jnp.where(causal, attn, 0.0).astype(BF16)
                o = (jnp.dot(attn, vh, preferred_element_type=F32)
                     + lax.dot_general(qm, st_b, nt, preferred_element_type=F32))
                o = _rms(o, nw)
                r = r_ref[rows, h * GLA_DV:(h + 1) * GLA_DV]
                o_ref[rows, h * GLA_DV:(h + 1) * GLA_DV] = (o * (r * jax.nn.sigmoid(r))).astype(BF16)
                ke_parts.append(jnp.where(half_mask[e], ke, 0.0).astype(BF16))
                v_parts.append(vh)
            upd = lax.dot_general(jnp.concatenate(v_parts, axis=0), jnp.concatenate(ke_parts, axis=0), tn,
                                  preferred_element_type=F32)
            st_ref[p] = st * decay[:, lanes] + upd


def _gla(gq, gk, glog, gv, gr, nw, batch, seq):
    ts = GLA_TS
    assert seq % ts == 0 and ts % CHUNK == 0
    n = seq // ts

    def row(b, s):
        return (b * n + s, 0)

    return pl.pallas_call(
        _gla_kernel,
        grid=(batch, n),
        in_specs=[
            pl.BlockSpec((ts, GLA_QK), row),
            pl.BlockSpec((ts, GLA_QK), row),
            pl.BlockSpec((ts, GLA_QK), row),
            pl.BlockSpec((ts, GLA_V), row),
            pl.BlockSpec((ts, GLA_V), row),
            pl.BlockSpec((1, GLA_DV), lambda b, s: (0, 0)),
        ],
        out_specs=pl.BlockSpec((ts, GLA_V), row),
        out_shape=jax.ShapeDtypeStruct((batch * seq, GLA_V), BF16),
        scratch_shapes=[pltpu.VMEM((GLA_HEADS // 2, GLA_DV, 2 * GLA_DK), F32)],
        compiler_params=pltpu.CompilerParams(dimension_semantics=("parallel", "arbitrary"),
                                             vmem_limit_bytes=VMEM_LIMIT),
        name="gla",
    )(gq, gk, glog, gv, gr, nw)


def _dsa_kernel(qi_ref, misc_ref, q_ref, ik_ref, k_ref, vt_ref, o_ref, sc_ref, acc_ref, qm_ref, lg_ref, *, n_sel):
    qb = qi_ref.shape[0]
    tk = sc_ref.shape[1]
    j = pl.program_id(1)
    n_tiles = ((j + 1) * qb + tk - 1) // tk
    nt = (((1,), (1,)), ((), ()))
    groups = tk // 8

    col = lax.broadcasted_iota(jnp.int32, (1, qb), 1)
    limit = j * qb + (col // CHUNK + 1) * CHUNK
    k_eff = jnp.minimum(limit, n_sel).astype(F32)
    key_in_tile = lax.broadcasted_iota(jnp.int32, (tk, qb), 0)
    lane = lax.broadcasted_iota(jnp.int32, (1, LANES), 1)

    misc_t = misc_ref[...].T
    wh = []
    for h in range(IDX_HEADS):
        pair = qi_ref[:, (h // 2) * LANES:(h // 2 + 1) * LANES].astype(F32)
        in_head = (lane // IDX_DIM) == (h % 2)
        qm_ref[h] = jnp.where(in_head, pair, 0.0).astype(BF16)
        wh.append(misc_t[M_IW + h:M_IW + h + 1, :])

    def for_each_tile(body, init):
        def pair_body(i, carry):
            return body(2 * i + 1, body(2 * i, carry))
        carry = lax.fori_loop(0, n_tiles // 2, pair_body, init)
        return lax.cond(n_tiles % 2 == 1, lambda c: body(n_tiles - 1, c), lambda c: c, carry)

    def fold_max(a):
        return jnp.max(a.reshape(groups, 8, a.shape[-1]), axis=0)

    def fold_min(a):
        return jnp.min(a.reshape(groups, 8, a.shape[-1]), axis=0)

    def fold_sum(a):
        return jnp.sum(a.reshape(groups, 8, a.shape[-1]), axis=0)

    def score_tile(t, carry):
        pmin, pmax = carry
        kt = ik_ref[pl.ds(pl.multiple_of(t * tk, tk), tk), :]
        sc = None
        for h in range(IDX_HEADS):
            s = lax.dot_general(kt, qm_ref[h], nt, preferred_element_type=F32)
            term = wh[h] * jnp.maximum(s, 0.0)
            sc = term if sc is None else sc + term
        scm = jnp.where(key_in_tile < limit - t * tk, sc, -jnp.inf)
        sc_ref[t] = scm
        return jnp.minimum(pmin, fold_min(sc)), jnp.maximum(pmax, fold_max(scm))

    pmin, pmax = for_each_tile(score_tile,
                               (jnp.full((8, qb), jnp.inf, F32), jnp.full((8, qb), -jnp.inf, F32)))
    vmin = jnp.min(pmin, axis=0, keepdims=True)
    vmax = jnp.max(pmax, axis=0, keepdims=True)

    def count_ge(thr):
        def body(t, acc):
            return acc + fold_sum(jnp.where(sc_ref[t] >= thr, 1.0, 0.0))
        acc = for_each_tile(body, jnp.zeros((8, qb), F32))
        return jnp.sum(acc, axis=0, keepdims=True)

    def max_below(bound):
        def body(t, acc):
            s = sc_ref[t]
            return jnp.maximum(acc, fold_max(jnp.where(s < bound, s, -jnp.inf)))
        acc = for_each_tile(body, jnp.full((8, qb), -jnp.inf, F32))
        return jnp.max(acc, axis=0, keepdims=True)

    def bisect(_, carry):
        lo, hb, hi, c_hi = carry
        mid = 0.5 * lo + 0.5 * hb
        c = count_ge(mid)
        up = c >= k_eff
        return (jnp.where(up, mid, lo), jnp.where(up, hb, mid),
                jnp.where(up, hi, mid), jnp.where(up, c_hi, c))

    lo, hb, hi, c_hi = lax.fori_loop(
        0, DSA_BISECT, bisect,
        (vmin, vmax, jnp.full((1, qb), jnp.inf, F32), jnp.zeros((1, qb), F32)))

    max_steps = n_tiles * tk

    def not_done(carry):
        return jnp.logical_and(carry[0] > 0, carry[1] < max_steps)

    def step(carry):
        _, it, hi, c_hi, thr, done = carry
        cand = max_below(hi)
        c = count_ge(cand)
        ok = jnp.logical_and(done < 0.5, c >= k_eff)
        thr = jnp.where(ok, cand, thr)
        done_new = jnp.where(ok, 1.0, done)
        moving = done_new < 0.5
        hi = jnp.where(moving, cand, hi)
        c_hi = jnp.where(moving, c, c_hi)
        remaining = jnp.sum(jnp.where(moving, 1.0, 0.0)).astype(jnp.int32)
        return remaining, it + 1, hi, c_hi, thr, done_new

    _, _, _, c_hi, thr, _ = lax.while_loop(
        not_done, step, (jnp.int32(1), jnp.int32(0), hi, c_hi, vmin, jnp.zeros((1, qb), F32)))
    n_tie = k_eff - c_hi

    ti = lax.broadcasted_iota(jnp.int32, (tk, tk), 0)
    tj = lax.broadcasted_iota(jnp.int32, (tk, tk), 1)
    tri_incl = jnp.where(tj <= ti, 1.0, 0.0).astype(BF16)
    neg = -1e30

    def logit_tile(t, carry):
        seen, pmax = carry
        s = sc_ref[t]
        tie = s == thr
        incl = jnp.dot(tri_incl, jnp.where(tie, 1.0, 0.0).astype(BF16), preferred_element_type=F32)
        sel = jnp.logical_or(s > thr, jnp.logical_and(tie, incl <= n_tie - seen))
        bias = jnp.where(sel, 0.0, neg)
        kt = k_ref[pl.ds(pl.multiple_of(t * tk, tk), tk), :]
        pmax_out = []
        for h in range(DSA_HEADS):
            lg = lax.dot_general(kt, q_ref[:, h * LANES:(h + 1) * LANES], nt, preferred_element_type=F32) + bias
            lg_ref[t, h] = lg
            pmax_out.append(jnp.maximum(pmax[h], fold_max(lg)))
        return seen + incl[tk - 1:tk, :], tuple(pmax_out)

    pmax0 = tuple(jnp.full((8, qb), neg, F32) for _ in range(DSA_HEADS))
    _, pmax = for_each_tile(logit_tile, (jnp.zeros((1, qb), F32), pmax0))
    m = [jnp.max(pmax[h], axis=0, keepdims=True) for h in range(DSA_HEADS)]

    acc_ref[...] = jnp.zeros_like(acc_ref)

    def value_tile(t, carry):
        vt = vt_ref[t]
        for h in range(DSA_HEADS):
            lanes = slice(h * qb, (h + 1) * qb)
            p = jnp.exp2(lg_ref[t, h] - m[h]).astype(BF16)
            acc_ref[:, lanes] += jnp.dot(vt, p, preferred_element_type=F32)
        return carry

    for_each_tile(value_tile, jnp.int32(0))
    for h in range(DSA_HEADS):
        lanes = slice(h * qb, (h + 1) * qb)
        out_t = acc_ref[:DSA_HEAD_DIM, lanes] / acc_ref[DSA_HEAD_DIM:DSA_HEAD_DIM + 1, lanes]
        o_ref[:, h * LANES:(h + 1) * LANES] = out_t.T.astype(BF16)


def _dsa(iq, misc, dq, ik2, dk, dvt, batch, seq):
    qb, tk = DSA_QB, DSA_TK
    assert seq % qb == 0 and seq % tk == 0 and qb % CHUNK == 0
    nq = seq // qb
    nkt = seq // tk
    n_sel = min(TOPK_MAX, seq // 4)

    def qrow(b, j):
        return (b * nq + j, 0)

    def kv(b, j):
        return (b, 0)

    return pl.pallas_call(
        functools.partial(_dsa_kernel, n_sel=n_sel),
        grid=(batch, nq),
        in_specs=[
            pl.BlockSpec((qb, IDX_Q), qrow),
            pl.BlockSpec((qb, LANES), qrow),
            pl.BlockSpec((qb, DSA_Q), qrow),
            pl.BlockSpec((seq, LANES), kv),
            pl.BlockSpec((seq, DSA_HEAD_DIM), kv),
            pl.BlockSpec((nkt, DSA_VROWS, tk), lambda b, j: (b, 0, 0)),
        ],
        out_specs=pl.BlockSpec((qb, DSA_Q), qrow),
        out_shape=jax.ShapeDtypeStruct((batch * seq, DSA_Q), BF16),
        scratch_shapes=[pltpu.VMEM((nkt, tk, qb), F32),
                        pltpu.VMEM((DSA_VROWS, DSA_HEADS * qb), F32),
                        pltpu.VMEM((IDX_HEADS, qb, LANES), BF16),
                        pltpu.VMEM((nkt, DSA_HEADS, tk, qb), F32)],
        compiler_params=pltpu.CompilerParams(dimension_semantics=("parallel", "arbitrary"),
                                             vmem_limit_bytes=VMEM_LIMIT),
        name="dsa",
    )(iq, misc, dq, ik2, dk, dvt)


def _out_ffn_kernel(gla_ref, dsa_ref, x_ref, wo_ref, n1_ref, n2_ref, w1_ref, w2_ref, n3_ref, o_ref):
    mixed = (jnp.dot(gla_ref[...], wo_ref[:GLA_V, :], preferred_element_type=F32)
             + jnp.dot(dsa_ref[...], wo_ref[GLA_V:, :], preferred_element_type=F32))
    x1 = x_ref[...] + _rms(mixed, n1_ref[...])
    h = _rms(x1, n2_ref[...]).astype(BF16)
    d_ff = w1_ref.shape[1]
    acc = jnp.zeros(x1.shape, F32)
    for f in range(d_ff // FFN_TF):
        cols = slice(f * FFN_TF, (f + 1) * FFN_TF)
        a = jnp.maximum(jnp.dot(h, w1_ref[:, cols], preferred_element_type=F32), 0.0)
        acc = acc + jnp.dot((a * a).astype(BF16), w2_ref[cols, :], preferred_element_type=F32)
    o_ref[...] = x1 + _rms(acc, n3_ref[...])


def _out_ffn(gla, dsa, x2, wo, n1, n2, w1, w2, n3):
    t, d = x2.shape
    tm = FFN_TM
    d_ff = w1.shape[1]
    assert t % tm == 0 and d_ff % FFN_TF == 0

    def row(i):
        return (i, 0)

    def const(i):
        return (0, 0)

    def resident(shape):
        return pl.BlockSpec(shape, const, pipeline_mode=pl.Buffered(1))

    return pl.pallas_call(
        _out_ffn_kernel,
        grid=(t // tm,),
        in_specs=[
            pl.BlockSpec((tm, GLA_V), row),
            pl.BlockSpec((tm, DSA_Q), row),
            pl.BlockSpec((tm, d), row),
            resident((GLA_V + DSA_Q, d)),
            resident((1, d)),
            resident((1, d)),
            resident((d, d_ff)),
            resident((d_ff, d)),
            resident((1, d)),
        ],
        out_specs=pl.BlockSpec((tm, d), row),
        out_shape=jax.ShapeDtypeStruct((t, d), F32),
        compiler_params=pltpu.CompilerParams(dimension_semantics=("parallel",), vmem_limit_bytes=VMEM_LIMIT),
        name="out_ffn",
    )(gla, dsa, x2, wo, n1, n2, w1, w2, n3)


def _rope_tables(seq):
    pos = jnp.arange(seq, dtype=F32)[:, None]

    def table(dim, reps):
        inv = ROPE_THETA ** (-jnp.arange(0, dim, 2, dtype=F32) / dim)
        ang = pos * inv[None, :]
        cos, sin = jnp.cos(ang), jnp.sin(ang)
        return (jnp.tile(jnp.concatenate([cos, cos], axis=-1), (1, reps)),
                jnp.tile(jnp.concatenate([-sin, sin], axis=-1), (1, reps)))

    cos128, sin128 = table(DSA_HEAD_DIM, 1)
    cos64, sin64 = table(IDX_DIM, LANES // IDX_DIM)
    return cos128, sin128, cos64, sin64


def _pack_w_in(w_in):
    sizes = (GLA_QK, GLA_QK, GLA_V, GLA_GATE_RANK, GLA_V, DSA_Q, DSA_HEAD_DIM, DSA_HEAD_DIM, IDX_Q, IDX_DIM, IDX_HEADS)
    offs = np.concatenate([[0], np.cumsum(sizes)])
    g_q, g_k, g_v, g_lr, g_r, d_q, d_k, d_v, i_q, i_k, i_w = [w_in[:, offs[i]:offs[i + 1]] for i in range(len(sizes))]
    pad = jnp.zeros((w_in.shape[0], LANES - IDX_DIM - IDX_HEADS - GLA_GATE_RANK), w_in.dtype)
    return jnp.concatenate([g_q, g_k, g_v, g_r, d_q, d_k, d_v, i_q, i_k, i_w, g_lr, pad], axis=1)


def kernel(x, norm_mix_pre, w_in, gla_gate_w2, gla_gate_b, gla_norm_w, idx_k_norm_w, idx_k_norm_b,
           w_out, norm_mix_post, norm_ffn_pre, w_ff1, w_ff2, norm_ffn_post):
    batch, seq, d = x.shape
    tabs = _rope_tables(seq)
    x2 = x.reshape(batch * seq, d)
    for l in range(w_in.shape[0]):
        w_pack = _pack_w_in(w_in[l]).astype(BF16)
        w2_pad = jnp.zeros((LANES, GLA_QK), F32).at[M_LR:M_LR + GLA_GATE_RANK].set(gla_gate_w2[l]).astype(BF16)
        lnw = jnp.zeros((1, LANES), F32).at[0, :IDX_DIM].set(idx_k_norm_w[l])
        lnb = jnp.zeros((1, LANES), F32).at[0, :IDX_DIM].set(idx_k_norm_b[l])
        (gq, gk, gv, gr, glog, dq, dk, dvt, iq, ik2, misc) = _projection(
            x2, norm_mix_pre[l][None], w_pack, w2_pad, gla_gate_b[l][None], lnw, lnb, tabs, seq)
        gla = _gla(gq, gk, glog, gv, gr, gla_norm_w[l][None], batch, seq)
        dsa = _dsa(iq, misc, dq, ik2, dk, dvt, batch, seq)
        x2 = _out_ffn(gla, dsa, x2, w_out[l].astype(BF16), norm_mix_post[l][None], norm_ffn_pre[l][None],
                      w_ff1[l].astype(BF16), w_ff2[l].astype(BF16), norm_ffn_post[l][None])
    return x2.reshape(batch, seq, d)
```

```python
import functools

import jax
import jax.numpy as jnp
import numpy as np
from jax import lax
from jax.experimental import pallas as pl
from jax.experimental.pallas import tpu as pltpu

F32 = jnp.float32
BF16 = jnp.bfloat16

EPS = 1e-6
ROPE_THETA = 10000.0
CHUNK = 64
GLA_HEADS = 4
GLA_DK = 64
GLA_DV = 128
GLA_GATE_RANK = 16
GLA_TAU = 16.0
DSA_HEADS = 4
DSA_HEAD_DIM = 128
IDX_HEADS = 4
IDX_DIM = 64
TOPK_MAX = 256

GLA_QK = GLA_HEADS * GLA_DK
GLA_V = GLA_HEADS * GLA_DV
DSA_Q = DSA_HEADS * DSA_HEAD_DIM
IDX_Q = IDX_HEADS * IDX_DIM

LANES = 128
VMEM_LIMIT = 56 * 1024 * 1024

C_GQ = 0
C_GK = C_GQ + GLA_QK
C_GV = C_GK + GLA_QK
C_GR = C_GV + GLA_V
C_DQ = C_GR + GLA_V
C_DK = C_DQ + DSA_Q
C_DV = C_DK + DSA_HEAD_DIM
C_IQ = C_DV + DSA_HEAD_DIM
C_MISC = C_IQ + IDX_Q
D_PACK = C_MISC + LANES
M_IW = IDX_DIM
M_LR = IDX_DIM + IDX_HEADS

PROJ_TM = 512
GLA_TS = 512
DSA_QB = 256
DSA_TK = 256
DSA_UNROLL = 4
DSA_BISECT_COARSE = 10
DSA_BISECT = 7
PACKED_ROWS = 16
BF16_STEP = 2.0 ** -7
TINY = 2.0 ** -120
DSA_ONES = PACKED_ROWS
DSA_VROWS = DSA_HEAD_DIM + DSA_ONES
DSA_Q_SCALE = DSA_HEAD_DIM ** -0.5 * 1.4426950408889634
FFN_TM = 512
FFN_TF = 1024


def _rms(x, w):
    return x * lax.rsqrt(jnp.mean(x * x, axis=-1, keepdims=True) + EPS) * w


def _proj_kernel(x_ref, nw_ref, w_ref, w2_ref, gb_ref, lnw_ref, lnb_ref,
                 cos128_ref, sin128_ref, cos64_ref, sin64_ref,
                 gq_ref, gk_ref, gv_ref, gr_ref, glog_ref,
                 dq_ref, dk_ref, dv_ref, iq_ref, ik_ref, misc_ref):
    hb = _rms(x_ref[...], nw_ref[...]).astype(BF16)

    def proj(lo, width):
        return jnp.dot(hb, w_ref[:, lo:lo + width], preferred_element_type=F32)

    cos128 = cos128_ref[...]
    sin128 = sin128_ref[...]
    cos64 = cos64_ref[...]
    sin64 = sin64_ref[...]
    lane = lax.broadcasted_iota(jnp.int32, (1, LANES), 1)
    first_half64 = (lane % IDX_DIM) < (IDX_DIM // 2)

    def rope128(t):
        return t * cos128 + pltpu.roll(t, LANES // 2, 1) * sin128

    def rope64(t):
        rot = jnp.where(first_half64, pltpu.roll(t, LANES - IDX_DIM // 2, 1), pltpu.roll(t, IDX_DIM // 2, 1))
        return t * cos64 + rot * sin64

    misc = proj(C_MISC, LANES)
    misc_ref[...] = misc * (IDX_HEADS ** -0.5 * IDX_DIM ** -0.5)

    is_ik = lane < IDX_DIM
    mu = jnp.sum(jnp.where(is_ik, misc, 0.0), axis=-1, keepdims=True) * (1.0 / IDX_DIM)
    d = jnp.where(is_ik, misc - mu, 0.0)
    var = jnp.sum(d * d, axis=-1, keepdims=True) * (1.0 / IDX_DIM)
    y = d * lax.rsqrt(var + EPS) * lnw_ref[...] + lnb_ref[...]
    yr = rope64(y)
    ik_ref[...] = (yr + pltpu.roll(yr, LANES // 2, 1)).astype(BF16)

    z = jnp.dot(misc.astype(BF16), w2_ref[...], preferred_element_type=F32) + gb_ref[...]
    glog_ref[...] = (jnp.minimum(z, 0.0) - jnp.log1p(jnp.exp(-jnp.abs(z)))) * (1.0 / GLA_TAU)

    dq = proj(C_DQ, DSA_Q)
    for h in range(DSA_HEADS):
        sl = slice(h * LANES, (h + 1) * LANES)
        dq_ref[:, sl] = (rope128(dq[:, sl]) * DSA_Q_SCALE).astype(BF16)
    kvq = proj(C_DK, 2 * DSA_HEAD_DIM + IDX_Q)
    dk_ref[...] = rope128(kvq[:, :DSA_HEAD_DIM]).astype(BF16)
    for c in range(IDX_Q // LANES):
        src = slice(C_IQ - C_DK + c * LANES, C_IQ - C_DK + (c + 1) * LANES)
        iq_ref[:, c * LANES:(c + 1) * LANES] = rope64(kvq[:, src]).astype(BF16)

    dv = kvq[:, C_DV - C_DK:C_IQ - C_DK]
    ones = jnp.ones((DSA_ONES, DSA_TK), BF16)
    for c in range(dv_ref.shape[0]):
        dv_t = dv[c * DSA_TK:(c + 1) * DSA_TK, :].T.astype(BF16)
        dv_ref[c] = jnp.concatenate([dv_t, ones], axis=0)

    gqk = proj(C_GQ, 2 * GLA_QK)
    gq_ref[...] = gqk[:, :GLA_QK]
    gk_ref[...] = gqk[:, GLA_QK:]
    gv_ref[...] = proj(C_GV, GLA_V).astype(BF16)
    gr_ref[...] = proj(C_GR, GLA_V)


def _projection(x2, nw, w_pack, w2_pad, gb, lnw, lnb, tabs, seq):
    t, d = x2.shape
    tm = PROJ_TM
    assert t % tm == 0 and seq % tm == 0
    tiles_per_seq = seq // tm

    def row(i):
        return (i, 0)

    def const(i):
        return (0, 0)

    def pos(i):
        return (i % tiles_per_seq, 0)

    widths = [(GLA_QK, F32), (GLA_QK, F32), (GLA_V, BF16), (GLA_V, F32), (GLA_QK, F32),
              (DSA_Q, BF16), (DSA_HEAD_DIM, BF16), (DSA_HEAD_DIM, BF16), (IDX_Q, BF16),
              (LANES, BF16), (LANES, F32)]
    DV_SLOT = 7
    assert tm % DSA_TK == 0
    dvt_spec = pl.BlockSpec((tm // DSA_TK, DSA_VROWS, DSA_TK), lambda i: (i, 0, 0))
    dvt_shape = jax.ShapeDtypeStruct((t // DSA_TK, DSA_VROWS, DSA_TK), BF16)
    return pl.pallas_call(
        _proj_kernel,
        grid=(t // tm,),
        in_specs=[
            pl.BlockSpec((tm, d), row),
            pl.BlockSpec((1, d), const),
            pl.BlockSpec((d, D_PACK), const),
            pl.BlockSpec((LANES, GLA_QK), const),
            pl.BlockSpec((1, GLA_QK), const),
            pl.BlockSpec((1, LANES), const),
            pl.BlockSpec((1, LANES), const),
            pl.BlockSpec((tm, LANES), pos),
            pl.BlockSpec((tm, LANES), pos),
            pl.BlockSpec((tm, LANES), pos),
            pl.BlockSpec((tm, LANES), pos),
        ],
        out_specs=[dvt_spec if n == DV_SLOT else pl.BlockSpec((tm, w), row) for n, (w, _) in enumerate(widths)],
        out_shape=[dvt_shape if n == DV_SLOT else jax.ShapeDtypeStruct((t, w), dt)
                   for n, (w, dt) in enumerate(widths)],
        compiler_params=pltpu.CompilerParams(dimension_semantics=("parallel",), vmem_limit_bytes=VMEM_LIMIT),
        name="proj",
    )(x2, nw, w_pack, w2_pad, gb, lnw, lnb, *tabs)


def _gla_kernel(q_ref, k_ref, g_ref, v_ref, r_ref, nw_ref, o_ref, st_ref):
    @pl.when(pl.program_id(1) == 0)
    def _():
        st_ref[...] = jnp.zeros_like(st_ref)

    ts = q_ref.shape[0]
    n_chunks = ts // CHUNK
    n_pairs = GLA_HEADS // 2
    ri = lax.broadcasted_iota(jnp.int32, (CHUNK, CHUNK), 0)
    ci = lax.broadcasted_iota(jnp.int32, (CHUNK, CHUNK), 1)
    causal = ri >= ci
    tri = jnp.where(causal, 1.0, 0.0).astype(BF16)
    lane = lax.broadcasted_iota(jnp.int32, (1, LANES), 1)
    half_mask = [lane < GLA_DK, lane >= GLA_DK]
    nw = nw_ref[...]
    nt = (((1,), (1,)), ((), ()))
    tn = (((0,), (0,)), ((), ()))
    chunks = [slice(c * CHUNK, (c + 1) * CHUNK) for c in range(n_chunks)]
    heads = [(p, e) for p in range(n_pairs) for e in range(2)]

    g = g_ref[...]
    g1 = g.astype(BF16)
    g2 = (g - g1.astype(F32)).astype(BF16)
    g3 = (g - g1.astype(F32) - g2.astype(F32)).astype(BF16)
    bcum = [jnp.dot(tri, g1[rows], preferred_element_type=F32)
            + jnp.dot(tri, g2[rows], preferred_element_type=F32)
            + jnp.dot(tri, g3[rows], preferred_element_type=F32) for rows in chunks]
    b_last = [b[CHUNK - 1:CHUNK, :] for b in bcum]
    decay = [jnp.exp(b) for b in b_last]
    q_dec = [q_ref[rows, :] * (GLA_DK ** -0.5) * jnp.exp(b) for rows, b in zip(chunks, bcum)]
    k_inv = [(k_ref[rows, :] * jnp.exp(-b)).astype(BF16) for rows, b in zip(chunks, bcum)]
    k_end = [k_ref[rows, :] * jnp.exp(bl - b) for rows, b, bl in zip(chunks, bcum, b_last)]

    def pair_lanes(x, p):
        return x[:, p * LANES:(p + 1) * LANES]

    qm = [[jnp.where(half_mask[e], pair_lanes(q_dec[c], p), 0.0).astype(BF16) for p, e in heads]
          for c in range(n_chunks)]
    attn = [[jnp.where(causal, lax.dot_general(qm[c][i], pair_lanes(k_inv[c], p), nt, preferred_element_type=F32),
                       0.0).astype(BF16) for i, (p, e) in enumerate(heads)] for c in range(n_chunks)]
    o_intra = [[jnp.dot(attn[c][i], v_ref[chunks[c], i * GLA_DV:(i + 1) * GLA_DV], preferred_element_type=F32)
                for i in range(GLA_HEADS)] for c in range(n_chunks)]
    upd = []
    for c in range(n_chunks):
        per_pair = []
        for p in range(n_pairs):
            ke = pair_lanes(k_end[c], p)
            ke2 = jnp.concatenate([jnp.where(half_mask[e], ke, 0.0).astype(BF16) for e in range(2)], axis=0)
            v2 = jnp.concatenate([v_ref[chunks[c], (2 * p + e) * GLA_DV:(2 * p + e + 1) * GLA_DV] for e in range(2)],
                                 axis=0)
            per_pair.append(lax.dot_general(v2, ke2, tn, preferred_element_type=F32))
        upd.append(per_pair)

    st_in = []
    for p in range(n_pairs):
        st = st_ref[p]
        per_chunk = []
        for c in range(n_chunks):
            per_chunk.append(st.astype(BF16))
            st = st * pair_lanes(decay[c], p) + upd[c][p]
        st_ref[p] = st
        st_in.append(per_chunk)

    for c in range(n_chunks):
        for i, (p, e) in enumerate(heads):
            o = o_intra[c][i] + lax.dot_general(qm[c][i], st_in[p][c], nt, preferred_element_type=F32)
            o = _rms(o, nw)
            r = r_ref[chunks[c], i * GLA_DV:(i + 1) * GLA_DV]
            o_ref[chunks[c], i * GLA_DV:(i + 1) * GLA_DV] = (o * (r * jax.nn.sigmoid(r))).astype(BF16)


def _gla(gq, gk, glog, gv, gr, nw, batch, seq):
    ts = GLA_TS
    assert seq % ts == 0 and ts % CHUNK == 0
    n = seq // ts

    def row(b, s):
        return (b * n + s, 0)

    return pl.pallas_call(
        _gla_kernel,
        grid=(batch, n),
        in_specs=[
            pl.BlockSpec((ts, GLA_QK), row),
            pl.BlockSpec((ts, GLA_QK), row),
            pl.BlockSpec((ts, GLA_QK), row),
            pl.BlockSpec((ts, GLA_V), row),
            pl.BlockSpec((ts, GLA_V), row),
            pl.BlockSpec((1, GLA_DV), lambda b, s: (0, 0)),
        ],
        out_specs=pl.BlockSpec((ts, GLA_V), row),
        out_shape=jax.ShapeDtypeStruct((batch * seq, GLA_V), BF16),
        scratch_shapes=[pltpu.VMEM((GLA_HEADS // 2, GLA_DV, 2 * GLA_DK), F32)],
        compiler_params=pltpu.CompilerParams(dimension_semantics=("parallel", "arbitrary"),
                                             vmem_limit_bytes=VMEM_LIMIT),
        name="gla",
    )(gq, gk, glog, gv, gr, nw)


def _dsa_kernel(qi_ref, misc_ref, q_ref, ik_ref, k_ref, vt_ref, o_ref,
                sc_ref, acc_ref, qm_ref, lg_ref, scb_ref, *, n_sel):
    qb = qi_ref.shape[0]
    tk = sc_ref.shape[1]
    j = pl.program_id(1)
    n_tiles = ((j + 1) * qb + tk - 1) // tk
    nt = (((1,), (1,)), ((), ()))
    groups = tk // 8

    col = lax.broadcasted_iota(jnp.int32, (1, qb), 1)
    limit = j * qb + (col // CHUNK + 1) * CHUNK
    k_eff = jnp.minimum(limit, n_sel).astype(F32)
    key_in_tile = lax.broadcasted_iota(jnp.int32, (tk, qb), 0)
    lane = lax.broadcasted_iota(jnp.int32, (1, LANES), 1)

    misc_t = misc_ref[...].T
    wh = []
    for h in range(IDX_HEADS):
        pair = qi_ref[:, (h // 2) * LANES:(h // 2 + 1) * LANES].astype(F32)
        in_head = (lane // IDX_DIM) == (h % 2)
        qm_ref[h] = jnp.where(in_head, pair, 0.0).astype(BF16)
        wh.append(misc_t[M_IW + h:M_IW + h + 1, :])

    def for_each_tile(body, init, unroll=2):
        def run(first, count, carry):
            for k in range(count):
                carry = body(first + k, carry)
            return carry
        carry = lax.fori_loop(0, n_tiles // unroll, lambda i, c: run(unroll * i, unroll, c), init)
        first = (n_tiles // unroll) * unroll
        tail = unroll // 2
        while tail >= 1:
            carry = lax.cond((n_tiles & tail) != 0, functools.partial(run, first, tail), lambda c: c, carry)
            first = first + (n_tiles & tail)
            tail //= 2
        return carry

    def fold_max(a):
        return jnp.max(a.reshape(groups, 8, a.shape[-1]), axis=0)

    def fold_min(a):
        return jnp.min(a.reshape(groups, 8, a.shape[-1]), axis=0)

    def fold_sum(a):
        return jnp.sum(a.reshape(groups, 8, a.shape[-1]), axis=0)

    def score_tile(t, carry):
        pmin, pmax = carry
        kt = ik_ref[pl.ds(pl.multiple_of(t * tk, tk), tk), :]
        sc = None
        for h in range(IDX_HEADS):
            s = lax.dot_general(kt, qm_ref[h], nt, preferred_element_type=F32)
            term = wh[h] * jnp.maximum(s, 0.0)
            sc = term if sc is None else sc + term
        scm = jnp.where(key_in_tile < limit - t * tk, sc, -jnp.inf)
        sc_ref[t] = scm
        scb_ref[t] = scm.astype(BF16)
        return jnp.minimum(pmin, fold_min(sc)), jnp.maximum(pmax, fold_max(scm))

    pmin, pmax = for_each_tile(score_tile,
                               (jnp.full((8, qb), jnp.inf, F32), jnp.full((8, qb), -jnp.inf, F32)), DSA_UNROLL)
    vmin = jnp.min(pmin, axis=0, keepdims=True)
    vmax = jnp.max(pmax, axis=0, keepdims=True)

    def count_ge(thr):
        def body(t, acc):
            return acc + fold_sum(jnp.where(sc_ref[t] >= thr, 1.0, 0.0))
        acc = for_each_tile(body, jnp.zeros((8, qb), F32))
        return jnp.sum(acc, axis=0, keepdims=True)

    def max_below(bound):
        def body(t, acc):
            s = sc_ref[t]
            return jnp.maximum(acc, fold_max(jnp.where(s < bound, s, -jnp.inf)))
        acc = for_each_tile(body, jnp.full((8, qb), -jnp.inf, F32))
        return jnp.max(acc, axis=0, keepdims=True)

    def count_ge_coarse(thr_b):
        one, zero = jnp.ones((), BF16), jnp.zeros((), BF16)

        def body(t, acc):
            ind = jnp.where(scb_ref[t] >= thr_b, one, zero)
            parts = [ind[r * PACKED_ROWS:(r + 1) * PACKED_ROWS, :] for r in range(tk // PACKED_ROWS)]
            while len(parts) > 1:
                parts = [a + b for a, b in zip(parts[::2], parts[1::2])]
            return acc + parts[0]
        acc = for_each_tile(body, jnp.zeros((PACKED_ROWS, qb), BF16))
        return jnp.sum(acc.astype(F32), axis=0, keepdims=True)

    def bisect_coarse(_, carry):
        lo, hb, hi = carry
        mid_b = (0.5 * lo + 0.5 * hb).astype(BF16)
        mid = mid_b.astype(F32)
        up = count_ge_coarse(mid_b) >= k_eff
        return jnp.where(up, mid, lo), jnp.where(up, hb, mid), jnp.where(up, hi, mid)

    lo, _, hi = lax.fori_loop(0, DSA_BISECT_COARSE, bisect_coarse,
                              (vmin, vmax, jnp.full((1, qb), jnp.inf, F32)))
    lo = lo - jnp.abs(lo) * BF16_STEP - TINY
    hi = hi + jnp.abs(hi) * BF16_STEP + TINY
    hb = jnp.minimum(hi, vmax)
    c_hi = count_ge(hi)

    def bisect(_, carry):
        lo, hb, hi, c_hi = carry
        mid = 0.5 * lo + 0.5 * hb
        c = count_ge(mid)
        up = c >= k_eff
        return (jnp.where(up, mid, lo), jnp.where(up, hb, mid),
                jnp.where(up, hi, mid), jnp.where(up, c_hi, c))

    lo, hb, hi, c_hi = lax.fori_loop(0, DSA_BISECT, bisect, (lo, hb, hi, c_hi))

    max_steps = n_tiles * tk

    def not_done(carry):
        return jnp.logical_and(carry[0] > 0, carry[1] < max_steps)

    def step(carry):
        _, it, hi, c_hi, thr, done = carry
        cand = max_below(hi)
        c = count_ge(cand)
        ok = jnp.logical_and(done < 0.5, c >= k_eff)
        thr = jnp.where(ok, cand, thr)
        done_new = jnp.where(ok, 1.0, done)
        moving = done_new < 0.5
        hi = jnp.where(moving, cand, hi)
        c_hi = jnp.where(moving, c, c_hi)
        remaining = jnp.sum(jnp.where(moving, 1.0, 0.0)).astype(jnp.int32)
        return remaining, it + 1, hi, c_hi, thr, done_new

    _, _, _, c_hi, thr, _ = lax.while_loop(
        not_done, step, (jnp.int32(1), jnp.int32(0), hi, c_hi, vmin, jnp.zeros((1, qb), F32)))
    n_tie = k_eff - c_hi

    ti = lax.broadcasted_iota(jnp.int32, (tk, tk), 0)
    tj = lax.broadcasted_iota(jnp.int32, (tk, tk), 1)
    tri_incl = jnp.where(tj <= ti, 1.0, 0.0).astype(BF16)
    neg = -1e30

    def logit_tile(t, carry):
        seen, pmax = carry
        s = sc_ref[t]
        tie = s == thr
        incl = jnp.dot(tri_incl, jnp.where(tie, 1.0, 0.0).astype(BF16), preferred_element_type=F32)
        sel = jnp.logical_or(s > thr, jnp.logical_and(tie, incl <= n_tie - seen))
        bias = jnp.where(sel, 0.0, neg)
        kt = k_ref[pl.ds(pl.multiple_of(t * tk, tk), tk), :]
        pmax_out = []
        for h in range(DSA_HEADS):
            lg = lax.dot_general(kt, q_ref[:, h * LANES:(h + 1) * LANES], nt, preferred_element_type=F32) + bias
            lg_ref[t, h] = lg
            pmax_out.append(jnp.maximum(pmax[h], fold_max(lg)))
        return seen + incl[tk - 1:tk, :], tuple(pmax_out)

    pmax0 = tuple(jnp.full((8, qb), neg, F32) for _ in range(DSA_HEADS))
    _, pmax = for_each_tile(logit_tile, (jnp.zeros((1, qb), F32), pmax0), DSA_UNROLL)
    m = [jnp.max(pmax[h], axis=0, keepdims=True) for h in range(DSA_HEADS)]

    acc_ref[...] = jnp.zeros_like(acc_ref)

    def value_tiles(tiles):
        vt = jnp.concatenate([vt_ref[t] for t in tiles], axis=1)
        for h in range(DSA_HEADS):
            lanes = slice(h * qb, (h + 1) * qb)
            p = jnp.concatenate([jnp.exp2((lg_ref[t, h] - m[h]).astype(BF16)) for t in tiles], axis=0)
            acc_ref[:, lanes] += jnp.dot(vt, p, preferred_element_type=F32)

    @pl.loop(0, n_tiles // DSA_UNROLL)
    def _(i):
        value_tiles([DSA_UNROLL * i + k for k in range(DSA_UNROLL)])

    first = (n_tiles // DSA_UNROLL) * DSA_UNROLL
    tail = DSA_UNROLL // 2
    while tail >= 1:
        pl.when((n_tiles & tail) != 0)(functools.partial(value_tiles, [first + k for k in range(tail)]))
        first = first + (n_tiles & tail)
        tail //= 2

    for h in range(DSA_HEADS):
        lanes = slice(h * qb, (h + 1) * qb)
        out_t = acc_ref[:DSA_HEAD_DIM, lanes] / acc_ref[DSA_HEAD_DIM:DSA_HEAD_DIM + 1, lanes]
        o_ref[:, h * LANES:(h + 1) * LANES] = out_t.T.astype(BF16)


def _dsa(iq, misc, dq, ik2, dk, dvt, batch, seq):
    qb, tk = DSA_QB, DSA_TK
    assert seq % qb == 0 and seq % tk == 0 and qb % CHUNK == 0
    nq = seq // qb
    nkt = seq // tk
    n_sel = min(TOPK_MAX, seq // 4)

    def qrow(b, j):
        return (b * nq + j, 0)

    def kv(b, j):
        return (b, 0)

    return pl.pallas_call(
        functools.partial(_dsa_kernel, n_sel=n_sel),
        grid=(batch, nq),
        in_specs=[
            pl.BlockSpec((qb, IDX_Q), qrow),
            pl.BlockSpec((qb, LANES), qrow),
            pl.BlockSpec((qb, DSA_Q), qrow),
            pl.BlockSpec((seq, LANES), kv),
            pl.BlockSpec((seq, DSA_HEAD_DIM), kv),
            pl.BlockSpec((nkt, DSA_VROWS, tk), lambda b, j: (b, 0, 0)),
        ],
        out_specs=pl.BlockSpec((qb, DSA_Q), qrow),
        out_shape=jax.ShapeDtypeStruct((batch * seq, DSA_Q), BF16),
        scratch_shapes=[pltpu.VMEM((nkt, tk, qb), F32),
                        pltpu.VMEM((DSA_VROWS, DSA_HEADS * qb), F32),
                        pltpu.VMEM((IDX_HEADS, qb, LANES), BF16),
                        pltpu.VMEM((nkt, DSA_HEADS, tk, qb), F32),
                        pltpu.VMEM((nkt, tk, qb), BF16)],
        compiler_params=pltpu.CompilerParams(dimension_semantics=("parallel", "arbitrary"),
                                             vmem_limit_bytes=VMEM_LIMIT),
        name="dsa",
    )(iq, misc, dq, ik2, dk, dvt)


def _out_ffn_kernel(gla_ref, dsa_ref, x_ref, wo_ref, n1_ref, n2_ref, w1_ref, w2_ref, n3_ref, o_ref):
    mixed = (jnp.dot(gla_ref[...], wo_ref[:GLA_V, :], preferred_element_type=F32)
             + jnp.dot(dsa_ref[...], wo_ref[GLA_V:, :], preferred_element_type=F32))
    x1 = x_ref[...] + _rms(mixed, n1_ref[...])
    h = _rms(x1, n2_ref[...]).astype(BF16)
    d_ff = w1_ref.shape[1]
    acc = jnp.zeros(x1.shape, F32)
    for f in range(d_ff // FFN_TF):
        cols = slice(f * FFN_TF, (f + 1) * FFN_TF)
        a = jnp.maximum(jnp.dot(h, w1_ref[:, cols], preferred_element_type=F32), 0.0)
        acc = acc + jnp.dot((a * a).astype(BF16), w2_ref[cols, :], preferred_element_type=F32)
    o_ref[...] = x1 + _rms(acc, n3_ref[...])


def _out_ffn(gla, dsa, x2, wo, n1, n2, w1, w2, n3):
    t, d = x2.shape
    tm = FFN_TM
    d_ff = w1.shape[1]
    assert t % tm == 0 and d_ff % FFN_TF == 0

    def row(i):
        return (i, 0)

    def const(i):
        return (0, 0)

    def resident(shape):
        return pl.BlockSpec(shape, const, pipeline_mode=pl.Buffered(1))

    return pl.pallas_call(
        _out_ffn_kernel,
        grid=(t // tm,),
        in_specs=[
            pl.BlockSpec((tm, GLA_V), row),
            pl.BlockSpec((tm, DSA_Q), row),
            pl.BlockSpec((tm, d), row),
            resident((GLA_V + DSA_Q, d)),
            resident((1, d)),
            resident((1, d)),
            resident((d, d_ff)),
            resident((d_ff, d)),
            resident((1, d)),
        ],
        out_specs=pl.BlockSpec((tm, d), row),
        out_shape=jax.ShapeDtypeStruct((t, d), F32),
        compiler_params=pltpu.CompilerParams(dimension_semantics=("parallel",), vmem_limit_bytes=VMEM_LIMIT),
        name="out_ffn",
    )(gla, dsa, x2, wo, n1, n2, w1, w2, n3)


def _rope_tables(seq):
    pos = jnp.arange(seq, dtype=F32)[:, None]

    def table(dim, reps):
        inv = ROPE_THETA ** (-jnp.arange(0, dim, 2, dtype=F32) / dim)
        ang = pos * inv[None, :]
        cos, sin = jnp.cos(ang), jnp.sin(ang)
        return (jnp.tile(jnp.concatenate([cos, cos], axis=-1), (1, reps)),
                jnp.tile(jnp.concatenate([-sin, sin], axis=-1), (1, reps)))

    cos128, sin128 = table(DSA_HEAD_DIM, 1)
    cos64, sin64 = table(IDX_DIM, LANES // IDX_DIM)
    return cos128, sin128, cos64, sin64


def _pack_w_in(w_in):
    sizes = (GLA_QK, GLA_QK, GLA_V, GLA_GATE_RANK, GLA_V, DSA_Q, DSA_HEAD_DIM, DSA_HEAD_DIM, IDX_Q, IDX_DIM, IDX_HEADS)
    offs = np.concatenate([[0], np.cumsum(sizes)])
    g_q, g_k, g_v, g_lr, g_r, d_q, d_k, d_v, i_q, i_k, i_w = [w_in[:, offs[i]:offs[i + 1]] for i in range(len(sizes))]
    pad = jnp.zeros((w_in.shape[0], LANES - IDX_DIM - IDX_HEADS - GLA_GATE_RANK), w_in.dtype)
    return jnp.concatenate([g_q, g_k, g_v, g_r, d_q, d_k, d_v, i_q, i_k, i_w, g_lr, pad], axis=1)


def kernel(x, norm_mix_pre, w_in, gla_gate_w2, gla_gate_b, gla_norm_w, idx_k_norm_w, idx_k_norm_b,
           w_out, norm_mix_post, norm_ffn_pre, w_ff1, w_ff2, norm_ffn_post):
    batch, seq, d = x.shape
    tabs = _rope_tables(seq)
    x2 = x.reshape(batch * seq, d)
    for l in range(w_in.shape[0]):
        w_pack = _pack_w_in(w_in[l]).astype(BF16)
        w2_pad = jnp.zeros((LANES, GLA_QK), F32).at[M_LR:M_LR + GLA_GATE_RANK].set(gla_gate_w2[l]).astype(BF16)
        lnw = jnp.zeros((1, LANES), F32).at[0, :IDX_DIM].set(idx_k_norm_w[l])
        lnb = jnp.zeros((1, LANES), F32).at[0, :IDX_DIM].set(idx_k_norm_b[l])
        (gq, gk, gv, gr, glog, dq, dk, dvt, iq, ik2, misc) = _projection(
            x2, norm_mix_pre[l][None], w_pack, w2_pad, gla_gate_b[l][None], lnw, lnb, tabs, seq)
        gla = _gla(gq, gk, glog, gv, gr, gla_norm_w[l][None], batch, seq)
        dsa = _dsa(iq, misc, dq, ik2, dk, dvt, batch, seq)
        x2 = _out_ffn(gla, dsa, x2, w_out[l].astype(BF16), norm_mix_post[l][None], norm_ffn_pre[l][None],
                      w_ff1[l].astype(BF16), w_ff2[l].astype(BF16), norm_ffn_post[l][None])
    return x2.reshape(batch, seq, d)
```

```python
import functools

import jax
import jax.numpy as jnp
import numpy as np
from jax import lax
from jax.experimental import pallas as pl
from jax.experimental.pallas import tpu as pltpu

F32 = jnp.float32
BF16 = jnp.bfloat16

EPS = 1e-6
ROPE_THETA = 10000.0
CHUNK = 64
GLA_HEADS = 4
GLA_DK = 64
GLA_DV = 128
GLA_GATE_RANK = 16
GLA_TAU = 16.0
DSA_HEADS = 4
DSA_HEAD_DIM = 128
IDX_HEADS = 4
IDX_DIM = 64
TOPK_MAX = 256

GLA_QK = GLA_HEADS * GLA_DK
GLA_V = GLA_HEADS * GLA_DV
DSA_Q = DSA_HEADS * DSA_HEAD_DIM
IDX_Q = IDX_HEADS * IDX_DIM

LANES = 128
VMEM_LIMIT = 56 * 1024 * 1024

C_GQ = 0
C_GK = C_GQ + GLA_QK
C_GV = C_GK + GLA_QK
C_GR = C_GV + GLA_V
C_DQ = C_GR + GLA_V
C_DK = C_DQ + DSA_Q
C_DV = C_DK + DSA_HEAD_DIM
C_IQ = C_DV + DSA_HEAD_DIM
C_MISC = C_IQ + IDX_Q
D_PACK = C_MISC + LANES
M_IW = IDX_DIM
M_LR = IDX_DIM + IDX_HEADS

PROJ_TM = 512
GLA_TS = 512
DSA_QB = 256
DSA_TK = 256
DSA_UNROLL = 4
DSA_BISECT_COARSE = 10
DSA_WALK_FIXED = 2
DSA_BISECT = 7
PACKED_ROWS = 16
BF16_STEP = 2.0 ** -7
TINY = 2.0 ** -120
DSA_ONES = PACKED_ROWS
DSA_VROWS = DSA_HEAD_DIM + DSA_ONES
DSA_Q_SCALE = DSA_HEAD_DIM ** -0.5 * 1.4426950408889634
FFN_TM = 512
FFN_TF = 1024


def _rms(x, w):
    return x * lax.rsqrt(jnp.mean(x * x, axis=-1, keepdims=True) + EPS) * w


def _proj_kernel(x_ref, nw_ref, w_ref, w2_ref, gb_ref, lnw_ref, lnb_ref,
                 cos128_ref, sin128_ref, cos64_ref, sin64_ref,
                 gq_ref, gk_ref, gv_ref, gr_ref, glog_ref,
                 dq_ref, dk_ref, dv_ref, iq_ref, ik_ref, misc_ref):
    hb = _rms(x_ref[...], nw_ref[...]).astype(BF16)

    def proj(lo, width):
        return jnp.dot(hb, w_ref[:, lo:lo + width], preferred_element_type=F32)

    cos128 = cos128_ref[...]
    sin128 = sin128_ref[...]
    cos64 = cos64_ref[...]
    sin64 = sin64_ref[...]
    lane = lax.broadcasted_iota(jnp.int32, (1, LANES), 1)
    first_half64 = (lane % IDX_DIM) < (IDX_DIM // 2)

    def rope128(t):
        return t * cos128 + pltpu.roll(t, LANES // 2, 1) * sin128

    def rope64(t):
        rot = jnp.where(first_half64, pltpu.roll(t, LANES - IDX_DIM // 2, 1), pltpu.roll(t, IDX_DIM // 2, 1))
        return t * cos64 + rot * sin64

    misc = proj(C_MISC, LANES)
    misc_ref[...] = misc * (IDX_HEADS ** -0.5 * IDX_DIM ** -0.5)

    is_ik = lane < IDX_DIM
    mu = jnp.sum(jnp.where(is_ik, misc, 0.0), axis=-1, keepdims=True) * (1.0 / IDX_DIM)
    d = jnp.where(is_ik, misc - mu, 0.0)
    var = jnp.sum(d * d, axis=-1, keepdims=True) * (1.0 / IDX_DIM)
    y = d * lax.rsqrt(var + EPS) * lnw_ref[...] + lnb_ref[...]
    yr = rope64(y)
    ik_ref[...] = (yr + pltpu.roll(yr, LANES // 2, 1)).astype(BF16)

    z = jnp.dot(misc.astype(BF16), w2_ref[...], preferred_element_type=F32) + gb_ref[...]
    glog_ref[...] = (jnp.minimum(z, 0.0) - jnp.log1p(jnp.exp(-jnp.abs(z)))) * (1.0 / GLA_TAU)

    dq = proj(C_DQ, DSA_Q)
    for h in range(DSA_HEADS):
        sl = slice(h * LANES, (h + 1) * LANES)
        dq_ref[:, sl] = (rope128(dq[:, sl]) * DSA_Q_SCALE).astype(BF16)
    kvq = proj(C_DK, 2 * DSA_HEAD_DIM + IDX_Q)
    dk_ref[...] = rope128(kvq[:, :DSA_HEAD_DIM]).astype(BF16)
    for c in range(IDX_Q // LANES):
        src = slice(C_IQ - C_DK + c * LANES, C_IQ - C_DK + (c + 1) * LANES)
        iq_ref[:, c * LANES:(c + 1) * LANES] = rope64(kvq[:, src]).astype(BF16)

    dv = kvq[:, C_DV - C_DK:C_IQ - C_DK]
    ones = jnp.ones((DSA_ONES, DSA_TK), BF16)
    for c in range(dv_ref.shape[0]):
        dv_t = dv[c * DSA_TK:(c + 1) * DSA_TK, :].T.astype(BF16)
        dv_ref[c] = jnp.concatenate([dv_t, ones], axis=0)

    gqk = proj(C_GQ, 2 * GLA_QK)
    gq_ref[...] = gqk[:, :GLA_QK]
    gk_ref[...] = gqk[:, GLA_QK:]
    gv_ref[...] = proj(C_GV, GLA_V).astype(BF16)
    gr_ref[...] = proj(C_GR, GLA_V)


def _projection(x2, nw, w_pack, w2_pad, gb, lnw, lnb, tabs, seq):
    t, d = x2.shape
    tm = PROJ_TM
    assert t % tm == 0 and seq % tm == 0
    tiles_per_seq = seq // tm

    def row(i):
        return (i, 0)

    def const(i):
        return (0, 0)

    def pos(i):
        return (i % tiles_per_seq, 0)

    widths = [(GLA_QK, F32), (GLA_QK, F32), (GLA_V, BF16), (GLA_V, F32), (GLA_QK, F32),
              (DSA_Q, BF16), (DSA_HEAD_DIM, BF16), (DSA_HEAD_DIM, BF16), (IDX_Q, BF16),
              (LANES, BF16), (LANES, F32)]
    DV_SLOT = 7
    assert tm % DSA_TK == 0
    dvt_spec = pl.BlockSpec((tm // DSA_TK, DSA_VROWS, DSA_TK), lambda i: (i, 0, 0))
    dvt_shape = jax.ShapeDtypeStruct((t // DSA_TK, DSA_VROWS, DSA_TK), BF16)
    return pl.pallas_call(
        _proj_kernel,
        grid=(t // tm,),
        in_specs=[
            pl.BlockSpec((tm, d), row),
            pl.BlockSpec((1, d), const),
            pl.BlockSpec((d, D_PACK), const),
            pl.BlockSpec((LANES, GLA_QK), const),
            pl.BlockSpec((1, GLA_QK), const),
            pl.BlockSpec((1, LANES), const),
            pl.BlockSpec((1, LANES), const),
            pl.BlockSpec((tm, LANES), pos),
            pl.BlockSpec((tm, LANES), pos),
            pl.BlockSpec((tm, LANES), pos),
            pl.BlockSpec((tm, LANES), pos),
        ],
        out_specs=[dvt_spec if n == DV_SLOT else pl.BlockSpec((tm, w), row) for n, (w, _) in enumerate(widths)],
        out_shape=[dvt_shape if n == DV_SLOT else jax.ShapeDtypeStruct((t, w), dt)
                   for n, (w, dt) in enumerate(widths)],
        compiler_params=pltpu.CompilerParams(dimension_semantics=("parallel",), vmem_limit_bytes=VMEM_LIMIT),
        name="proj",
    )(x2, nw, w_pack, w2_pad, gb, lnw, lnb, *tabs)


def _gla_kernel(q_ref, k_ref, g_ref, v_ref, r_ref, nw_ref, o_ref, st_ref):
    @pl.when(pl.program_id(1) == 0)
    def _():
        st_ref[...] = jnp.zeros_like(st_ref)

    ts = q_ref.shape[0]
    n_chunks = ts // CHUNK
    n_pairs = GLA_HEADS // 2
    ri = lax.broadcasted_iota(jnp.int32, (CHUNK, CHUNK), 0)
    ci = lax.broadcasted_iota(jnp.int32, (CHUNK, CHUNK), 1)
    causal = ri >= ci
    tri = jnp.where(causal, 1.0, 0.0).astype(BF16)
    lane = lax.broadcasted_iota(jnp.int32, (1, LANES), 1)
    half_mask = [lane < GLA_DK, lane >= GLA_DK]
    nw = nw_ref[...]
    nt = (((1,), (1,)), ((), ()))
    tn = (((0,), (0,)), ((), ()))
    chunks = [slice(c * CHUNK, (c + 1) * CHUNK) for c in range(n_chunks)]
    heads = [(p, e) for p in range(n_pairs) for e in range(2)]

    g = g_ref[...]
    g1 = g.astype(BF16)
    g2 = (g - g1.astype(F32)).astype(BF16)
    g3 = (g - g1.astype(F32) - g2.astype(F32)).astype(BF16)
    bcum = [jnp.dot(tri, g1[rows], preferred_element_type=F32)
            + jnp.dot(tri, g2[rows], preferred_element_type=F32)
            + jnp.dot(tri, g3[rows], preferred_element_type=F32) for rows in chunks]
    b_last = [b[CHUNK - 1:CHUNK, :] for b in bcum]
    decay = [jnp.exp(b) for b in b_last]
    q_dec = [q_ref[rows, :] * (GLA_DK ** -0.5) * jnp.exp(b) for rows, b in zip(chunks, bcum)]
    k_inv = [(k_ref[rows, :] * jnp.exp(-b)).astype(BF16) for rows, b in zip(chunks, bcum)]
    k_end = [k_ref[rows, :] * jnp.exp(bl - b) for rows, b, bl in zip(chunks, bcum, b_last)]

    def pair_lanes(x, p):
        return x[:, p * LANES:(p + 1) * LANES]

    qm = [[jnp.where(half_mask[e], pair_lanes(q_dec[c], p), 0.0).astype(BF16) for p, e in heads]
          for c in range(n_chunks)]
    attn = [[jnp.where(causal, lax.dot_general(qm[c][i], pair_lanes(k_inv[c], p), nt, preferred_element_type=F32),
                       0.0).astype(BF16) for i, (p, e) in enumerate(heads)] for c in range(n_chunks)]
    o_intra = [[jnp.dot(attn[c][i], v_ref[chunks[c], i * GLA_DV:(i + 1) * GLA_DV], preferred_element_type=F32)
                for i in range(GLA_HEADS)] for c in range(n_chunks)]
    upd = []
    for c in range(n_chunks):
        per_pair = []
        for p in range(n_pairs):
            ke = pair_lanes(k_end[c], p)
            ke2 = jnp.concatenate([jnp.where(half_mask[e], ke, 0.0).astype(BF16) for e in range(2)], axis=0)
            v2 = jnp.concatenate([v_ref[chunks[c], (2 * p + e) * GLA_DV:(2 * p + e + 1) * GLA_DV] for e in range(2)],
                                 axis=0)
            per_pair.append(lax.dot_general(v2, ke2, tn, preferred_element_type=F32))
        upd.append(per_pair)

    st_in = []
    for p in range(n_pairs):
        st = st_ref[p]
        per_chunk = []
        for c in range(n_chunks):
            per_chunk.append(st.astype(BF16))
            st = st * pair_lanes(decay[c], p) + upd[c][p]
        st_ref[p] = st
        st_in.append(per_chunk)

    for c in range(n_chunks):
        for i, (p, e) in enumerate(heads):
            o = o_intra[c][i] + lax.dot_general(qm[c][i], st_in[p][c], nt, preferred_element_type=F32)
            o = _rms(o, nw)
            r = r_ref[chunks[c], i * GLA_DV:(i + 1) * GLA_DV]
            o_ref[chunks[c], i * GLA_DV:(i + 1) * GLA_DV] = (o * (r * jax.nn.sigmoid(r))).astype(BF16)


def _gla(gq, gk, glog, gv, gr, nw, batch, seq):
    ts = GLA_TS
    assert seq % ts == 0 and ts % CHUNK == 0
    n = seq // ts

    def row(b, s):
        return (b * n + s, 0)

    return pl.pallas_call(
        _gla_kernel,
        grid=(batch, n),
        in_specs=[
            pl.BlockSpec((ts, GLA_QK), row),
            pl.BlockSpec((ts, GLA_QK), row),
            pl.BlockSpec((ts, GLA_QK), row),
            pl.BlockSpec((ts, GLA_V), row),
            pl.BlockSpec((ts, GLA_V), row),
            pl.BlockSpec((1, GLA_DV), lambda b, s: (0, 0)),
        ],
        out_specs=pl.BlockSpec((ts, GLA_V), row),
        out_shape=jax.ShapeDtypeStruct((batch * seq, GLA_V), BF16),
        scratch_shapes=[pltpu.VMEM((GLA_HEADS // 2, GLA_DV, 2 * GLA_DK), F32)],
        compiler_params=pltpu.CompilerParams(dimension_semantics=("parallel", "arbitrary"),
                                             vmem_limit_bytes=VMEM_LIMIT),
        name="gla",
    )(gq, gk, glog, gv, gr, nw)


def _dsa_kernel(qi_ref, misc_ref, q_ref, ik_ref, k_ref, vt_ref, o_ref,
                sc_ref, acc_ref, qm_ref, lg_ref, scb_ref, *, n_sel):
    qb = qi_ref.shape[0]
    tk = sc_ref.shape[1]
    j = pl.program_id(1)
    n_tiles = ((j + 1) * qb + tk - 1) // tk
    nt = (((1,), (1,)), ((), ()))

    col = lax.broadcasted_iota(jnp.int32, (1, qb), 1)
    limit = j * qb + (col // CHUNK + 1) * CHUNK
    k_eff = jnp.minimum(limit, n_sel).astype(F32)
    key_in_tile = lax.broadcasted_iota(jnp.int32, (tk, qb), 0)
    lane = lax.broadcasted_iota(jnp.int32, (1, LANES), 1)

    misc_t = misc_ref[...].T
    wh = []
    for h in range(IDX_HEADS):
        pair = qi_ref[:, (h // 2) * LANES:(h // 2 + 1) * LANES].astype(F32)
        in_head = (lane // IDX_DIM) == (h % 2)
        qm_ref[h] = jnp.where(in_head, pair, 0.0).astype(BF16)
        wh.append(misc_t[M_IW + h:M_IW + h + 1, :])

    def for_each_tile_group(run, init, unroll):
        carry = lax.fori_loop(0, n_tiles // unroll, lambda i, c: run(unroll * i, unroll, c), init)
        first = (n_tiles // unroll) * unroll
        tail = unroll // 2
        while tail >= 1:
            carry = lax.cond((n_tiles & tail) != 0, functools.partial(run, first, tail), lambda c: c, carry)
            first = first + (n_tiles & tail)
            tail //= 2
        return carry

    def for_each_tile(body, init, unroll=2):
        def run(first, count, carry):
            for k in range(count):
                carry = body(first + k, carry)
            return carry
        return for_each_tile_group(run, init, unroll)

    def fold(op, a):
        return op(a.reshape(a.shape[0] // 8, 8, a.shape[-1]), axis=0)

    def fold_max(a):
        return fold(jnp.max, a)

    def fold_sum(a):
        return fold(jnp.sum, a)

    def key_rows(ref, first, count):
        return ref[pl.ds(pl.multiple_of(first * tk, tk), count * tk), :]

    def score_tile(t, carry):
        pmin, pmax = carry
        kt = key_rows(ik_ref, t, 1)
        sc = None
        for h in range(IDX_HEADS):
            s = lax.dot_general(kt, qm_ref[h], nt, preferred_element_type=F32)
            term = wh[h] * jnp.maximum(s, 0.0)
            sc = term if sc is None else sc + term
        scm = jnp.where(key_in_tile < limit - t * tk, sc, -jnp.inf)
        sc_ref[t] = scm
        scb_ref[t] = scm.astype(BF16)
        return jnp.minimum(pmin, fold(jnp.min, sc)), jnp.maximum(pmax, fold_max(scm))

    pmin, pmax = for_each_tile(
        score_tile, (jnp.full((8, qb), jnp.inf, F32), jnp.full((8, qb), -jnp.inf, F32)), DSA_UNROLL)
    vmin = jnp.min(pmin, axis=0, keepdims=True)
    vmax = jnp.max(pmax, axis=0, keepdims=True)

    def count_ge(thr):
        def body(t, acc):
            return acc + fold_sum(jnp.where(sc_ref[t] >= thr, 1.0, 0.0))
        acc = for_each_tile(body, jnp.zeros((8, qb), F32))
        return jnp.sum(acc, axis=0, keepdims=True)

    def max_below(bound):
        def body(t, acc):
            s = sc_ref[t]
            return jnp.maximum(acc, fold_max(jnp.where(s < bound, s, -jnp.inf)))
        acc = for_each_tile(body, jnp.full((8, qb), -jnp.inf, F32))
        return jnp.max(acc, axis=0, keepdims=True)

    def count_ge_coarse(thr_b):
        one, zero = jnp.ones((), BF16), jnp.zeros((), BF16)

        def body(t, acc):
            ind = jnp.where(scb_ref[t] >= thr_b, one, zero)
            parts = [ind[r * PACKED_ROWS:(r + 1) * PACKED_ROWS, :] for r in range(tk // PACKED_ROWS)]
            while len(parts) > 1:
                parts = [a + b for a, b in zip(parts[::2], parts[1::2])]
            return acc + parts[0]
        acc = for_each_tile(body, jnp.zeros((PACKED_ROWS, qb), BF16))
        return jnp.sum(acc.astype(F32), axis=0, keepdims=True)

    def bisect_coarse(_, carry):
        lo, hb, hi = carry
        mid_b = (0.5 * lo + 0.5 * hb).astype(BF16)
        mid = mid_b.astype(F32)
        up = count_ge_coarse(mid_b) >= k_eff
        return jnp.where(up, mid, lo), jnp.where(up, hb, mid), jnp.where(up, hi, mid)

    lo, _, hi = lax.fori_loop(0, DSA_BISECT_COARSE, bisect_coarse,
                              (vmin, vmax, jnp.full((1, qb), jnp.inf, F32)))
    lo = lo - jnp.abs(lo) * BF16_STEP - TINY
    hi = hi + jnp.abs(hi) * BF16_STEP + TINY
    hb = jnp.minimum(hi, vmax)
    c_hi = count_ge(hi)

    def bisect(_, carry):
        lo, hb, hi, c_hi = carry
        mid = 0.5 * lo + 0.5 * hb
        c = count_ge(mid)
        up = c >= k_eff
        return (jnp.where(up, mid, lo), jnp.where(up, hb, mid),
                jnp.where(up, hi, mid), jnp.where(up, c_hi, c))

    lo, hb, hi, c_hi = lax.fori_loop(0, DSA_BISECT, bisect, (lo, hb, hi, c_hi))

    max_steps = n_tiles * tk

    def walk_step(state):
        hi, c_hi, thr, done = state
        cand = max_below(hi)
        c = count_ge(cand)
        ok = jnp.logical_and(done < 0.5, c >= k_eff)
        thr = jnp.where(ok, cand, thr)
        done = jnp.where(ok, 1.0, done)
        moving = done < 0.5
        return jnp.where(moving, cand, hi), jnp.where(moving, c, c_hi), thr, done

    def remaining(state):
        return jnp.sum(jnp.where(state[3] < 0.5, 1.0, 0.0)).astype(jnp.int32)

    state = (hi, c_hi, vmin, jnp.zeros((1, qb), F32))
    for _ in range(DSA_WALK_FIXED):
        state = walk_step(state)

    def not_done(carry):
        return jnp.logical_and(carry[0] > 0, carry[1] < max_steps)

    def step(carry):
        state = walk_step(carry[2])
        return remaining(state), carry[1] + 1, state

    _, _, (_, c_hi, thr, _) = lax.while_loop(not_done, step, (remaining(state), jnp.int32(0), state))
    n_tie = k_eff - c_hi

    ti = lax.broadcasted_iota(jnp.int32, (tk, tk), 0)
    tj = lax.broadcasted_iota(jnp.int32, (tk, tk), 1)
    tri_incl = jnp.where(tj <= ti, 1.0, 0.0).astype(BF16)
    neg = -1e30

    def logit_tile(t, carry):
        seen, pmax = carry
        s = sc_ref[t]
        tie = s == thr
        incl = jnp.dot(tri_incl, jnp.where(tie, 1.0, 0.0).astype(BF16), preferred_element_type=F32)
        sel = jnp.logical_or(s > thr, jnp.logical_and(tie, incl <= n_tie - seen))
        bias = jnp.where(sel, 0.0, neg)
        kt = key_rows(k_ref, t, 1)
        pmax_out = []
        for h in range(DSA_HEADS):
            lg = lax.dot_general(kt, q_ref[:, h * LANES:(h + 1) * LANES], nt, preferred_element_type=F32) + bias
            lg_ref[t, h] = lg
            pmax_out.append(jnp.maximum(pmax[h], fold_max(lg)))
        return seen + incl[tk - 1:tk, :], tuple(pmax_out)

    pmax0 = tuple(jnp.full((8, qb), neg, F32) for _ in range(DSA_HEADS))
    _, pmax = for_each_tile(logit_tile, (jnp.zeros((1, qb), F32), pmax0), DSA_UNROLL)
    m = [jnp.max(pmax[h], axis=0, keepdims=True) for h in range(DSA_HEADS)]

    acc_ref[...] = jnp.zeros_like(acc_ref)

    def value_tiles(tiles):
        vt = jnp.concatenate([vt_ref[t] for t in tiles], axis=1)
        for h in range(DSA_HEADS):
            lanes = slice(h * qb, (h + 1) * qb)
            p = jnp.concatenate([jnp.exp2((lg_ref[t, h] - m[h]).astype(BF16)) for t in tiles], axis=0)
            acc_ref[:, lanes] += jnp.dot(vt, p, preferred_element_type=F32)

    @pl.loop(0, n_tiles // DSA_UNROLL)
    def _(i):
        value_tiles([DSA_UNROLL * i + k for k in range(DSA_UNROLL)])

    first = (n_tiles // DSA_UNROLL) * DSA_UNROLL
    tail = DSA_UNROLL // 2
    while tail >= 1:
        pl.when((n_tiles & tail) != 0)(functools.partial(value_tiles, [first + k for k in range(tail)]))
        first = first + (n_tiles & tail)
        tail //= 2

    for h in range(DSA_HEADS):
        lanes = slice(h * qb, (h + 1) * qb)
        out_t = acc_ref[:DSA_HEAD_DIM, lanes] / acc_ref[DSA_HEAD_DIM:DSA_HEAD_DIM + 1, lanes]
        o_ref[:, h * LANES:(h + 1) * LANES] = out_t.T.astype(BF16)


def _dsa(iq, misc, dq, ik2, dk, dvt, batch, seq):
    qb, tk = DSA_QB, DSA_TK
    assert seq % qb == 0 and seq % tk == 0 and qb % CHUNK == 0
    nq = seq // qb
    nkt = seq // tk
    n_sel = min(TOPK_MAX, seq // 4)

    def qrow(b, j):
        return (b * nq + j, 0)

    def kv(b, j):
        return (b, 0)

    return pl.pallas_call(
        functools.partial(_dsa_kernel, n_sel=n_sel),
        grid=(batch, nq),
        in_specs=[
            pl.BlockSpec((qb, IDX_Q), qrow),
            pl.BlockSpec((qb, LANES), qrow),
            pl.BlockSpec((qb, DSA_Q), qrow),
            pl.BlockSpec((seq, LANES), kv),
            pl.BlockSpec((seq, DSA_HEAD_DIM), kv),
            pl.BlockSpec((nkt, DSA_VROWS, tk), lambda b, j: (b, 0, 0)),
        ],
        out_specs=pl.BlockSpec((qb, DSA_Q), qrow),
        out_shape=jax.ShapeDtypeStruct((batch * seq, DSA_Q), BF16),
        scratch_shapes=[pltpu.VMEM((nkt, tk, qb), F32),
                        pltpu.VMEM((DSA_VROWS, DSA_HEADS * qb), F32),
                        pltpu.VMEM((IDX_HEADS, qb, LANES), BF16),
                        pltpu.VMEM((nkt, DSA_HEADS, tk, qb), F32),
                        pltpu.VMEM((nkt, tk, qb), BF16)],
        compiler_params=pltpu.CompilerParams(dimension_semantics=("parallel", "arbitrary"),
                                             vmem_limit_bytes=VMEM_LIMIT),
        name="dsa",
    )(iq, misc, dq, ik2, dk, dvt)


def _out_ffn_kernel(gla_ref, dsa_ref, x_ref, wo_ref, n1_ref, n2_ref, w1_ref, w2_ref, n3_ref, o_ref):
    mixed = (jnp.dot(gla_ref[...], wo_ref[:GLA_V, :], preferred_element_type=F32)
             + jnp.dot(dsa_ref[...], wo_ref[GLA_V:, :], preferred_element_type=F32))
    x1 = x_ref[...] + _rms(mixed, n1_ref[...])
    h = _rms(x1, n2_ref[...]).astype(BF16)
    d_ff = w1_ref.shape[1]
    acc = jnp.zeros(x1.shape, F32)
    for f in range(d_ff // FFN_TF):
        cols = slice(f * FFN_TF, (f + 1) * FFN_TF)
        a = jnp.maximum(jnp.dot(h, w1_ref[:, cols], preferred_element_type=F32), 0.0)
        acc = acc + jnp.dot((a * a).astype(BF16), w2_ref[cols, :], preferred_element_type=F32)
    o_ref[...] = x1 + _rms(acc, n3_ref[...])


def _out_ffn(gla, dsa, x2, wo, n1, n2, w1, w2, n3):
    t, d = x2.shape
    tm = FFN_TM
    d_ff = w1.shape[1]
    assert t % tm == 0 and d_ff % FFN_TF == 0

    def row(i):
        return (i, 0)

    def const(i):
        return (0, 0)

    def resident(shape):
        return pl.BlockSpec(shape, const, pipeline_mode=pl.Buffered(1))

    return pl.pallas_call(
        _out_ffn_kernel,
        grid=(t // tm,),
        in_specs=[
            pl.BlockSpec((tm, GLA_V), row),
            pl.BlockSpec((tm, DSA_Q), row),
            pl.BlockSpec((tm, d), row),
            resident((GLA_V + DSA_Q, d)),
            resident((1, d)),
            resident((1, d)),
            resident((d, d_ff)),
            resident((d_ff, d)),
            resident((1, d)),
        ],
        out_specs=pl.BlockSpec((tm, d), row),
        out_shape=jax.ShapeDtypeStruct((t, d), F32),
        compiler_params=pltpu.CompilerParams(dimension_semantics=("parallel",), vmem_limit_bytes=VMEM_LIMIT),
        name="out_ffn",
    )(gla, dsa, x2, wo, n1, n2, w1, w2, n3)


def _rope_tables(seq):
    pos = jnp.arange(seq, dtype=F32)[:, None]

    def table(dim, reps):
        inv = ROPE_THETA ** (-jnp.arange(0, dim, 2, dtype=F32) / dim)
        ang = pos * inv[None, :]
        cos, sin = jnp.cos(ang), jnp.sin(ang)
        return (jnp.tile(jnp.concatenate([cos, cos], axis=-1), (1, reps)),
                jnp.tile(jnp.concatenate([-sin, sin], axis=-1), (1, reps)))

    cos128, sin128 = table(DSA_HEAD_DIM, 1)
    cos64, sin64 = table(IDX_DIM, LANES // IDX_DIM)
    return cos128, sin128, cos64, sin64


def _pack_w_in(w_in):
    sizes = (GLA_QK, GLA_QK, GLA_V, GLA_GATE_RANK, GLA_V, DSA_Q, DSA_HEAD_DIM, DSA_HEAD_DIM, IDX_Q, IDX_DIM, IDX_HEADS)
    offs = np.concatenate([[0], np.cumsum(sizes)])
    g_q, g_k, g_v, g_lr, g_r, d_q, d_k, d_v, i_q, i_k, i_w = [w_in[:, offs[i]:offs[i + 1]] for i in range(len(sizes))]
    pad = jnp.zeros((w_in.shape[0], LANES - IDX_DIM - IDX_HEADS - GLA_GATE_RANK), w_in.dtype)
    return jnp.concatenate([g_q, g_k, g_v, g_r, d_q, d_k, d_v, i_q, i_k, i_w, g_lr, pad], axis=1)


def kernel(x, norm_mix_pre, w_in, gla_gate_w2, gla_gate_b, gla_norm_w, idx_k_norm_w, idx_k_norm_b,
           w_out, norm_mix_post, norm_ffn_pre, w_ff1, w_ff2, norm_ffn_post):
    batch, seq, d = x.shape
    tabs = _rope_tables(seq)
    x2 = x.reshape(batch * seq, d)
    for l in range(w_in.shape[0]):
        w_pack = _pack_w_in(w_in[l]).astype(BF16)
        w2_pad = jnp.zeros((LANES, GLA_QK), F32).at[M_LR:M_LR + GLA_GATE_RANK].set(gla_gate_w2[l]).astype(BF16)
        lnw = jnp.zeros((1, LANES), F32).at[0, :IDX_DIM].set(idx_k_norm_w[l])
        lnb = jnp.zeros((1, LANES), F32).at[0, :IDX_DIM].set(idx_k_norm_b[l])
        (gq, gk, gv, gr, glog, dq, dk, dvt, iq, ik2, misc) = _projection(
            x2, norm_mix_pre[l][None], w_pack, w2_pad, gla_gate_b[l][None], lnw, lnb, tabs, seq)
        gla = _gla(gq, gk, glog, gv, gr, gla_norm_w[l][None], batch, seq)
        dsa = _dsa(iq, misc, dq, ik2, dk, dvt, batch, seq)
        x2 = _out_ffn(gla, dsa, x2, w_out[l].astype(BF16), norm_mix_post[l][None], norm_ffn_pre[l][None],
                      w_ff1[l].astype(BF16), w_ff2[l].astype(BF16), norm_ffn_post[l][None])
    return x2.reshape(batch, seq, d)
```

```python
import functools

import jax
import jax.numpy as jnp
import numpy as np
from jax import lax
from jax.experimental import pallas as pl
from jax.experimental.pallas import tpu as pltpu

F32 = jnp.float32
BF16 = jnp.bfloat16

EPS = 1e-6
ROPE_THETA = 10000.0
CHUNK = 64
GLA_HEADS = 4
GLA_DK = 64
GLA_DV = 128
GLA_GATE_RANK = 16
GLA_TAU = 16.0
DSA_HEADS = 4
DSA_HEAD_DIM = 128
IDX_HEADS = 4
IDX_DIM = 64
TOPK_MAX = 256

GLA_QK = GLA_HEADS * GLA_DK
GLA_V = GLA_HEADS * GLA_DV
DSA_Q = DSA_HEADS * DSA_HEAD_DIM
IDX_Q = IDX_HEADS * IDX_DIM

LANES = 128
VMEM_LIMIT = 56 * 1024 * 1024

C_GQ = 0
C_GK = C_GQ + GLA_QK
C_GV = C_GK + GLA_QK
C_GR = C_GV + GLA_V
C_DQ = C_GR + GLA_V
C_DK = C_DQ + DSA_Q
C_DV = C_DK + DSA_HEAD_DIM
C_IQ = C_DV + DSA_HEAD_DIM
C_MISC = C_IQ + IDX_Q
D_PACK = C_MISC + LANES
M_IW = IDX_DIM
M_LR = IDX_DIM + IDX_HEADS

PROJ_TM = 512
GLA_TS = 512
DSA_QB = 256
DSA_TK = 256
DSA_SEQS = 2
DSA_UNROLL = 4
DSA_UNROLL_SCORE = 2
DSA_BISECT_COARSE = 10
DSA_WALK_FIXED = 2
DSA_BISECT = 7
PACKED_ROWS = 16
BF16_STEP = 2.0 ** -7
TINY = 2.0 ** -120
DSA_ONES = PACKED_ROWS
DSA_VROWS = DSA_HEAD_DIM + DSA_ONES
DSA_Q_SCALE = DSA_HEAD_DIM ** -0.5 * 1.4426950408889634
FFN_TM = 512
FFN_TF = 1024


def _rms(x, w):
    return x * lax.rsqrt(jnp.mean(x * x, axis=-1, keepdims=True) + EPS) * w


def _proj_kernel(x_ref, nw_ref, w_ref, w2_ref, gb_ref, lnw_ref, lnb_ref,
                 cos128_ref, sin128_ref, cos64_ref, sin64_ref,
                 gq_ref, gk_ref, gv_ref, gr_ref, glog_ref,
                 dq_ref, dk_ref, dv_ref, iq_ref, ik_ref, misc_ref):
    hb = _rms(x_ref[...], nw_ref[...]).astype(BF16)

    def proj(lo, width):
        return jnp.dot(hb, w_ref[:, lo:lo + width], preferred_element_type=F32)

    cos128 = cos128_ref[...]
    sin128 = sin128_ref[...]
    cos64 = cos64_ref[...]
    sin64 = sin64_ref[...]
    lane = lax.broadcasted_iota(jnp.int32, (1, LANES), 1)
    first_half64 = (lane % IDX_DIM) < (IDX_DIM // 2)

    def rope128(t):
        return t * cos128 + pltpu.roll(t, LANES // 2, 1) * sin128

    def rope64(t):
        rot = jnp.where(first_half64, pltpu.roll(t, LANES - IDX_DIM // 2, 1), pltpu.roll(t, IDX_DIM // 2, 1))
        return t * cos64 + rot * sin64

    misc = proj(C_MISC, LANES)
    misc_ref[...] = misc * (IDX_HEADS ** -0.5 * IDX_DIM ** -0.5)

    is_ik = lane < IDX_DIM
    mu = jnp.sum(jnp.where(is_ik, misc, 0.0), axis=-1, keepdims=True) * (1.0 / IDX_DIM)
    d = jnp.where(is_ik, misc - mu, 0.0)
    var = jnp.sum(d * d, axis=-1, keepdims=True) * (1.0 / IDX_DIM)
    y = d * lax.rsqrt(var + EPS) * lnw_ref[...] + lnb_ref[...]
    yr = rope64(y)
    ik_ref[...] = (yr + pltpu.roll(yr, LANES // 2, 1)).astype(BF16)

    z = jnp.dot(misc.astype(BF16), w2_ref[...], preferred_element_type=F32) + gb_ref[...]
    glog_ref[...] = (jnp.minimum(z, 0.0) - jnp.log1p(jnp.exp(-jnp.abs(z)))) * (1.0 / GLA_TAU)

    dq = proj(C_DQ, DSA_Q)
    for h in range(DSA_HEADS):
        sl = slice(h * LANES, (h + 1) * LANES)
        dq_ref[:, sl] = (rope128(dq[:, sl]) * DSA_Q_SCALE).astype(BF16)
    kvq = proj(C_DK, 2 * DSA_HEAD_DIM + IDX_Q)
    dk_ref[...] = rope128(kvq[:, :DSA_HEAD_DIM]).astype(BF16)
    for c in range(IDX_Q // LANES):
        src = slice(C_IQ - C_DK + c * LANES, C_IQ - C_DK + (c + 1) * LANES)
        iq_ref[:, c * LANES:(c + 1) * LANES] = rope64(kvq[:, src]).astype(BF16)

    dv = kvq[:, C_DV - C_DK:C_IQ - C_DK]
    ones = jnp.ones((DSA_ONES, DSA_TK), BF16)
    for c in range(dv_ref.shape[0]):
        dv_t = dv[c * DSA_TK:(c + 1) * DSA_TK, :].T.astype(BF16)
        dv_ref[c] = jnp.concatenate([dv_t, ones], axis=0)

    gqk = proj(C_GQ, 2 * GLA_QK)
    gq_ref[...] = gqk[:, :GLA_QK]
    gk_ref[...] = gqk[:, GLA_QK:]
    gv_ref[...] = proj(C_GV, GLA_V).astype(BF16)
    gr_ref[...] = proj(C_GR, GLA_V)


def _projection(x2, nw, w_pack, w2_pad, gb, lnw, lnb, tabs, seq):
    t, d = x2.shape
    tm = PROJ_TM
    assert t % tm == 0 and seq % tm == 0
    tiles_per_seq = seq // tm

    def row(i):
        return (i, 0)

    def const(i):
        return (0, 0)

    def pos(i):
        return (i % tiles_per_seq, 0)

    widths = [(GLA_QK, F32), (GLA_QK, F32), (GLA_V, BF16), (GLA_V, F32), (GLA_QK, F32),
              (DSA_Q, BF16), (DSA_HEAD_DIM, BF16), (DSA_HEAD_DIM, BF16), (IDX_Q, BF16),
              (LANES, BF16), (LANES, F32)]
    DV_SLOT = 7
    assert tm % DSA_TK == 0
    dvt_spec = pl.BlockSpec((tm // DSA_TK, DSA_VROWS, DSA_TK), lambda i: (i, 0, 0))
    dvt_shape = jax.ShapeDtypeStruct((t // DSA_TK, DSA_VROWS, DSA_TK), BF16)
    return pl.pallas_call(
        _proj_kernel,
        grid=(t // tm,),
        in_specs=[
            pl.BlockSpec((tm, d), row),
            pl.BlockSpec((1, d), const),
            pl.BlockSpec((d, D_PACK), const),
            pl.BlockSpec((LANES, GLA_QK), const),
            pl.BlockSpec((1, GLA_QK), const),
            pl.BlockSpec((1, LANES), const),
            pl.BlockSpec((1, LANES), const),
            pl.BlockSpec((tm, LANES), pos),
            pl.BlockSpec((tm, LANES), pos),
            pl.BlockSpec((tm, LANES), pos),
            pl.BlockSpec((tm, LANES), pos),
        ],
        out_specs=[dvt_spec if n == DV_SLOT else pl.BlockSpec((tm, w), row) for n, (w, _) in enumerate(widths)],
        out_shape=[dvt_shape if n == DV_SLOT else jax.ShapeDtypeStruct((t, w), dt)
                   for n, (w, dt) in enumerate(widths)],
        compiler_params=pltpu.CompilerParams(dimension_semantics=("parallel",), vmem_limit_bytes=VMEM_LIMIT),
        name="proj",
    )(x2, nw, w_pack, w2_pad, gb, lnw, lnb, *tabs)


def _gla_kernel(q_ref, k_ref, g_ref, v_ref, r_ref, nw_ref, o_ref, st_ref):
    @pl.when(pl.program_id(1) == 0)
    def _():
        st_ref[...] = jnp.zeros_like(st_ref)

    ts = q_ref.shape[0]
    n_chunks = ts // CHUNK
    n_pairs = GLA_HEADS // 2
    ri = lax.broadcasted_iota(jnp.int32, (CHUNK, CHUNK), 0)
    ci = lax.broadcasted_iota(jnp.int32, (CHUNK, CHUNK), 1)
    causal = ri >= ci
    tri = jnp.where(causal, 1.0, 0.0).astype(BF16)
    lane = lax.broadcasted_iota(jnp.int32, (1, LANES), 1)
    half_mask = [lane < GLA_DK, lane >= GLA_DK]
    nw = nw_ref[...]
    nt = (((1,), (1,)), ((), ()))
    tn = (((0,), (0,)), ((), ()))
    chunks = [slice(c * CHUNK, (c + 1) * CHUNK) for c in range(n_chunks)]
    heads = [(p, e) for p in range(n_pairs) for e in range(2)]

    g = g_ref[...]
    g1 = g.astype(BF16)
    g2 = (g - g1.astype(F32)).astype(BF16)
    g3 = (g - g1.astype(F32) - g2.astype(F32)).astype(BF16)
    bcum = [jnp.dot(tri, g1[rows], preferred_element_type=F32)
            + jnp.dot(tri, g2[rows], preferred_element_type=F32)
            + jnp.dot(tri, g3[rows], preferred_element_type=F32) for rows in chunks]
    b_last = [b[CHUNK - 1:CHUNK, :] for b in bcum]
    decay = [jnp.exp(b) for b in b_last]
    q_dec = [q_ref[rows, :] * (GLA_DK ** -0.5) * jnp.exp(b) for rows, b in zip(chunks, bcum)]
    k_inv = [(k_ref[rows, :] * jnp.exp(-b)).astype(BF16) for rows, b in zip(chunks, bcum)]
    k_end = [k_ref[rows, :] * jnp.exp(bl - b) for rows, b, bl in zip(chunks, bcum, b_last)]

    def pair_lanes(x, p):
        return x[:, p * LANES:(p + 1) * LANES]

    qm = [[jnp.where(half_mask[e], pair_lanes(q_dec[c], p), 0.0).astype(BF16) for p, e in heads]
          for c in range(n_chunks)]
    attn = [[jnp.where(causal, lax.dot_general(qm[c][i], pair_lanes(k_inv[c], p), nt, preferred_element_type=F32),
                       0.0).astype(BF16) for i, (p, e) in enumerate(heads)] for c in range(n_chunks)]
    o_intra = [[jnp.dot(attn[c][i], v_ref[chunks[c], i * GLA_DV:(i + 1) * GLA_DV], preferred_element_type=F32)
                for i in range(GLA_HEADS)] for c in range(n_chunks)]
    upd = []
    for c in range(n_chunks):
        per_pair = []
        for p in range(n_pairs):
            ke = pair_lanes(k_end[c], p)
            ke2 = jnp.concatenate([jnp.where(half_mask[e], ke, 0.0).astype(BF16) for e in range(2)], axis=0)
            v2 = jnp.concatenate([v_ref[chunks[c], (2 * p + e) * GLA_DV:(2 * p + e + 1) * GLA_DV] for e in range(2)],
                                 axis=0)
            per_pair.append(lax.dot_general(v2, ke2, tn, preferred_element_type=F32))
        upd.append(per_pair)

    st_in = []
    for p in range(n_pairs):
        st = st_ref[p]
        per_chunk = []
        for c in range(n_chunks):
            per_chunk.append(st.astype(BF16))
            st = st * pair_lanes(decay[c], p) + upd[c][p]
        st_ref[p] = st
        st_in.append(per_chunk)

    for c in range(n_chunks):
        for i, (p, e) in enumerate(heads):
            o = o_intra[c][i] + lax.dot_general(qm[c][i], st_in[p][c], nt, preferred_element_type=F32)
            o = _rms(o, nw)
            r = r_ref[chunks[c], i * GLA_DV:(i + 1) * GLA_DV]
            o_ref[chunks[c], i * GLA_DV:(i + 1) * GLA_DV] = (o * (r * jax.nn.sigmoid(r))).astype(BF16)


def _gla(gq, gk, glog, gv, gr, nw, batch, seq):
    ts = GLA_TS
    assert seq % ts == 0 and ts % CHUNK == 0
    n = seq // ts

    def row(b, s):
        return (b * n + s, 0)

    return pl.pallas_call(
        _gla_kernel,
        grid=(batch, n),
        in_specs=[
            pl.BlockSpec((ts, GLA_QK), row),
            pl.BlockSpec((ts, GLA_QK), row),
            pl.BlockSpec((ts, GLA_QK), row),
            pl.BlockSpec((ts, GLA_V), row),
            pl.BlockSpec((ts, GLA_V), row),
            pl.BlockSpec((1, GLA_DV), lambda b, s: (0, 0)),
        ],
        out_specs=pl.BlockSpec((ts, GLA_V), row),
        out_shape=jax.ShapeDtypeStruct((batch * seq, GLA_V), BF16),
        scratch_shapes=[pltpu.VMEM((GLA_HEADS // 2, GLA_DV, 2 * GLA_DK), F32)],
        compiler_params=pltpu.CompilerParams(dimension_semantics=("parallel", "arbitrary"),
                                             vmem_limit_bytes=VMEM_LIMIT),
        name="gla",
    )(gq, gk, glog, gv, gr, nw)


def _dsa_kernel(qi_ref, misc_ref, q_ref, ik_ref, k_ref, vt_ref, o_ref,
                sc_ref, acc_ref, qm_ref, lg_ref, scb_ref, *, n_sel):
    nb, qb = qi_ref.shape[0], qi_ref.shape[1]
    width = nb * qb
    tk = sc_ref.shape[1]
    j = pl.program_id(1)
    n_tiles = ((j + 1) * qb + tk - 1) // tk
    nt = (((1,), (1,)), ((), ()))
    seq_lanes = [slice(b * qb, (b + 1) * qb) for b in range(nb)]

    col = lax.broadcasted_iota(jnp.int32, (1, width), 1) % qb
    limit = j * qb + (col // CHUNK + 1) * CHUNK
    k_eff = jnp.minimum(limit, n_sel).astype(F32)
    key_in_tile = lax.broadcasted_iota(jnp.int32, (tk, qb), 0)
    limit_q = limit[:, :qb]
    lane = lax.broadcasted_iota(jnp.int32, (1, LANES), 1)

    wh = []
    for b in range(nb):
        misc_t = misc_ref[b].T
        for h in range(IDX_HEADS):
            pair = qi_ref[b, :, (h // 2) * LANES:(h // 2 + 1) * LANES].astype(F32)
            in_head = (lane // IDX_DIM) == (h % 2)
            qm_ref[b, h] = jnp.where(in_head, pair, 0.0).astype(BF16)
        wh.append([misc_t[M_IW + h:M_IW + h + 1, :] for h in range(IDX_HEADS)])

    def for_each_tile_group(run, init, unroll):
        carry = lax.fori_loop(0, n_tiles // unroll, lambda i, c: run(unroll * i, unroll, c), init)
        first = (n_tiles // unroll) * unroll
        tail = unroll // 2
        while tail >= 1:
            carry = lax.cond((n_tiles & tail) != 0, functools.partial(run, first, tail), lambda c: c, carry)
            first = first + (n_tiles & tail)
            tail //= 2
        return carry

    def for_each_tile(body, init, unroll=2):
        def run(first, count, carry):
            for k in range(count):
                carry = body(first + k, carry)
            return carry
        return for_each_tile_group(run, init, unroll)

    def fold(op, a):
        return op(a.reshape(a.shape[0] // 8, 8, a.shape[-1]), axis=0)

    def fold_max(a):
        return fold(jnp.max, a)

    def fold_sum(a):
        return fold(jnp.sum, a)

    def key_rows(ref, b, t):
        return ref[b, pl.ds(pl.multiple_of(t * tk, tk), tk), :]

    def score_tile(t, carry):
        pmin, pmax = carry
        mins, maxs = [], []
        for b in range(nb):
            kt = key_rows(ik_ref, b, t)
            sc = None
            for h in range(IDX_HEADS):
                s = lax.dot_general(kt, qm_ref[b, h], nt, preferred_element_type=F32)
                term = wh[b][h] * jnp.maximum(s, 0.0)
                sc = term if sc is None else sc + term
            scm = jnp.where(key_in_tile < limit_q - t * tk, sc, -jnp.inf)
            sc_ref[t, :, seq_lanes[b]] = scm
            scb_ref[t, :, seq_lanes[b]] = scm.astype(BF16)
            mins.append(fold(jnp.min, sc))
            maxs.append(fold_max(scm))
        return (jnp.minimum(pmin, jnp.concatenate(mins, axis=1)),
                jnp.maximum(pmax, jnp.concatenate(maxs, axis=1)))

    pmin, pmax = for_each_tile(
        score_tile, (jnp.full((8, width), jnp.inf, F32), jnp.full((8, width), -jnp.inf, F32)), DSA_UNROLL_SCORE)
    vmin = jnp.min(pmin, axis=0, keepdims=True)
    vmax = jnp.max(pmax, axis=0, keepdims=True)

    def count_ge(thr):
        def body(t, acc):
            return acc + fold_sum(jnp.where(sc_ref[t] >= thr, 1.0, 0.0))
        acc = for_each_tile(body, jnp.zeros((8, width), F32))
        return jnp.sum(acc, axis=0, keepdims=True)

    def max_below(bound):
        def body(t, acc):
            s = sc_ref[t]
            return jnp.maximum(acc, fold_max(jnp.where(s < bound, s, -jnp.inf)))
        acc = for_each_tile(body, jnp.full((8, width), -jnp.inf, F32))
        return jnp.max(acc, axis=0, keepdims=True)

    def count_ge_coarse(thr_b):
        one, zero = jnp.ones((), BF16), jnp.zeros((), BF16)

        def body(t, acc):
            ind = jnp.where(scb_ref[t] >= thr_b, one, zero)
            parts = [ind[r * PACKED_ROWS:(r + 1) * PACKED_ROWS, :] for r in range(tk // PACKED_ROWS)]
            while len(parts) > 1:
                parts = [a + b for a, b in zip(parts[::2], parts[1::2])]
            return acc + parts[0]
        acc = for_each_tile(body, jnp.zeros((PACKED_ROWS, width), BF16))
        return jnp.sum(acc.astype(F32), axis=0, keepdims=True)

    def bisect_coarse(_, carry):
        lo, hb, hi = carry
        mid_b = (0.5 * lo + 0.5 * hb).astype(BF16)
        mid = mid_b.astype(F32)
        up = count_ge_coarse(mid_b) >= k_eff
        return jnp.where(up, mid, lo), jnp.where(up, hb, mid), jnp.where(up, hi, mid)

    lo, _, hi = lax.fori_loop(0, DSA_BISECT_COARSE, bisect_coarse,
                              (vmin, vmax, jnp.full((1, width), jnp.inf, F32)))
    lo = lo - jnp.abs(lo) * BF16_STEP - TINY
    hi = hi + jnp.abs(hi) * BF16_STEP + TINY
    hb = jnp.minimum(hi, vmax)
    c_hi = count_ge(hi)

    def bisect(_, carry):
        lo, hb, hi, c_hi = carry
        mid = 0.5 * lo + 0.5 * hb
        c = count_ge(mid)
        up = c >= k_eff
        return (jnp.where(up, mid, lo), jnp.where(up, hb, mid),
                jnp.where(up, hi, mid), jnp.where(up, c_hi, c))

    lo, hb, hi, c_hi = lax.fori_loop(0, DSA_BISECT, bisect, (lo, hb, hi, c_hi))

    max_steps = n_tiles * tk

    def walk_step(state):
        hi, c_hi, thr, done = state
        cand = max_below(hi)
        c = count_ge(cand)
        ok = jnp.logical_and(done < 0.5, c >= k_eff)
        thr = jnp.where(ok, cand, thr)
        done = jnp.where(ok, 1.0, done)
        moving = done < 0.5
        return jnp.where(moving, cand, hi), jnp.where(moving, c, c_hi), thr, done

    def remaining(state):
        return jnp.sum(jnp.where(state[3] < 0.5, 1.0, 0.0)).astype(jnp.int32)

    state = (hi, c_hi, vmin, jnp.zeros((1, width), F32))
    for _ in range(DSA_WALK_FIXED):
        state = walk_step(state)

    def not_done(carry):
        return jnp.logical_and(carry[0] > 0, carry[1] < max_steps)

    def step(carry):
        state = walk_step(carry[2])
        return remaining(state), carry[1] + 1, state

    _, _, (_, c_hi, thr, _) = lax.while_loop(not_done, step, (remaining(state), jnp.int32(0), state))
    n_tie = k_eff - c_hi

    ti = lax.broadcasted_iota(jnp.int32, (tk, tk), 0)
    tj = lax.broadcasted_iota(jnp.int32, (tk, tk), 1)
    tri_incl = jnp.where(tj <= ti, 1.0, 0.0).astype(BF16)
    neg = -1e30

    def attend(b):
        thr_b, n_tie_b = thr[:, seq_lanes[b]], n_tie[:, seq_lanes[b]]

        def logit_tile(t, carry):
            seen, pmax = carry
            s = sc_ref[t, :, seq_lanes[b]]
            tie = s == thr_b
            incl = jnp.dot(tri_incl, jnp.where(tie, 1.0, 0.0).astype(BF16), preferred_element_type=F32)
            sel = jnp.logical_or(s > thr_b, jnp.logical_and(tie, incl <= n_tie_b - seen))
            bias = jnp.where(sel, 0.0, neg)
            kt = key_rows(k_ref, b, t)
            pmax_out = []
            for h in range(DSA_HEADS):
                lg = lax.dot_general(kt, q_ref[b, :, h * LANES:(h + 1) * LANES], nt,
                                     preferred_element_type=F32) + bias
                lg_ref[t, h] = lg
                pmax_out.append(jnp.maximum(pmax[h], fold_max(lg)))
            return seen + incl[tk - 1:tk, :], tuple(pmax_out)

        pmax0 = tuple(jnp.full((8, qb), neg, F32) for _ in range(DSA_HEADS))
        _, pmax = for_each_tile(logit_tile, (jnp.zeros((1, qb), F32), pmax0), DSA_UNROLL)
        m = [jnp.max(pmax[h], axis=0, keepdims=True) for h in range(DSA_HEADS)]

        acc_ref[...] = jnp.zeros_like(acc_ref)

        def value_tiles(tiles):
            vt = jnp.concatenate([vt_ref[b, t] for t in tiles], axis=1)
            for h in range(DSA_HEADS):
                lanes = slice(h * qb, (h + 1) * qb)
                p = jnp.concatenate([jnp.exp2((lg_ref[t, h] - m[h]).astype(BF16)) for t in tiles], axis=0)
                acc_ref[:, lanes] += jnp.dot(vt, p, preferred_element_type=F32)

        @pl.loop(0, n_tiles // DSA_UNROLL)
        def _(i):
            value_tiles([DSA_UNROLL * i + k for k in range(DSA_UNROLL)])

        first = (n_tiles // DSA_UNROLL) * DSA_UNROLL
        tail = DSA_UNROLL // 2
        while tail >= 1:
            pl.when((n_tiles & tail) != 0)(functools.partial(value_tiles, [first + k for k in range(tail)]))
            first = first + (n_tiles & tail)
            tail //= 2

        for h in range(DSA_HEADS):
            lanes = slice(h * qb, (h + 1) * qb)
            out_t = acc_ref[:DSA_HEAD_DIM, lanes] / acc_ref[DSA_HEAD_DIM:DSA_HEAD_DIM + 1, lanes]
            o_ref[b, :, h * LANES:(h + 1) * LANES] = out_t.T.astype(BF16)

    for b in range(nb):
        attend(b)


def _dsa(iq, misc, dq, ik2, dk, dvt, batch, seq):
    qb, tk = DSA_QB, DSA_TK
    assert seq % qb == 0 and seq % tk == 0 and qb % CHUNK == 0
    nq = seq // qb
    nkt = seq // tk
    n_sel = min(TOPK_MAX, seq // 4)

    nb = DSA_SEQS if batch % DSA_SEQS == 0 else 1

    def per_seq(a):
        return a.reshape((batch, seq) + a.shape[1:])

    def qrow(b, j):
        return (b, j, 0)

    def kv(b, j):
        return (b, 0, 0)

    out = pl.pallas_call(
        functools.partial(_dsa_kernel, n_sel=n_sel),
        grid=(batch // nb, nq),
        in_specs=[
            pl.BlockSpec((nb, qb, IDX_Q), qrow),
            pl.BlockSpec((nb, qb, LANES), qrow),
            pl.BlockSpec((nb, qb, DSA_Q), qrow),
            pl.BlockSpec((nb, seq, LANES), kv),
            pl.BlockSpec((nb, seq, DSA_HEAD_DIM), kv),
            pl.BlockSpec((nb, nkt, DSA_VROWS, tk), lambda b, j: (b, 0, 0, 0)),
        ],
        out_specs=pl.BlockSpec((nb, qb, DSA_Q), qrow),
        out_shape=jax.ShapeDtypeStruct((batch, seq, DSA_Q), BF16),
        scratch_shapes=[pltpu.VMEM((nkt, tk, nb * qb), F32),
                        pltpu.VMEM((DSA_VROWS, DSA_HEADS * qb), F32),
                        pltpu.VMEM((nb, IDX_HEADS, qb, LANES), BF16),
                        pltpu.VMEM((nkt, DSA_HEADS, tk, qb), F32),
                        pltpu.VMEM((nkt, tk, nb * qb), BF16)],
        compiler_params=pltpu.CompilerParams(dimension_semantics=("parallel", "arbitrary"),
                                             vmem_limit_bytes=VMEM_LIMIT),
        name="dsa",
    )(per_seq(iq), per_seq(misc), per_seq(dq), per_seq(ik2), per_seq(dk),
      dvt.reshape((batch, nkt) + dvt.shape[1:]))
    return out.reshape(batch * seq, DSA_Q)


def _out_ffn_kernel(gla_ref, dsa_ref, x_ref, wo_ref, n1_ref, n2_ref, w1_ref, w2_ref, n3_ref, o_ref):
    mixed = (jnp.dot(gla_ref[...], wo_ref[:GLA_V, :], preferred_element_type=F32)
             + jnp.dot(dsa_ref[...], wo_ref[GLA_V:, :], preferred_element_type=F32))
    x1 = x_ref[...] + _rms(mixed, n1_ref[...])
    h = _rms(x1, n2_ref[...]).astype(BF16)
    d_ff = w1_ref.shape[1]
    acc = jnp.zeros(x1.shape, F32)
    for f in range(d_ff // FFN_TF):
        cols = slice(f * FFN_TF, (f + 1) * FFN_TF)
        a = jnp.maximum(jnp.dot(h, w1_ref[:, cols], preferred_element_type=F32), 0.0)
        acc = acc + jnp.dot((a * a).astype(BF16), w2_ref[cols, :], preferred_element_type=F32)
    o_ref[...] = x1 + _rms(acc, n3_ref[...])


def _out_ffn(gla, dsa, x2, wo, n1, n2, w1, w2, n3):
    t, d = x2.shape
    tm = FFN_TM
    d_ff = w1.shape[1]
    assert t % tm == 0 and d_ff % FFN_TF == 0

    def row(i):
        return (i, 0)

    def const(i):
        return (0, 0)

    def resident(shape):
        return pl.BlockSpec(shape, const, pipeline_mode=pl.Buffered(1))

    return pl.pallas_call(
        _out_ffn_kernel,
        grid=(t // tm,),
        in_specs=[
            pl.BlockSpec((tm, GLA_V), row),
            pl.BlockSpec((tm, DSA_Q), row),
            pl.BlockSpec((tm, d), row),
            resident((GLA_V + DSA_Q, d)),
            resident((1, d)),
            resident((1, d)),
            resident((d, d_ff)),
            resident((d_ff, d)),
            resident((1, d)),
        ],
        out_specs=pl.BlockSpec((tm, d), row),
        out_shape=jax.ShapeDtypeStruct((t, d), F32),
        compiler_params=pltpu.CompilerParams(dimension_semantics=("parallel",), vmem_limit_bytes=VMEM_LIMIT),
        name="out_ffn",
    )(gla, dsa, x2, wo, n1, n2, w1, w2, n3)


def _rope_tables(seq):
    pos = jnp.arange(seq, dtype=F32)[:, None]

    def table(dim, reps):
        inv = ROPE_THETA ** (-jnp.arange(0, dim, 2, dtype=F32) / dim)
        ang = pos * inv[None, :]
        cos, sin = jnp.cos(ang), jnp.sin(ang)
        return (jnp.tile(jnp.concatenate([cos, cos], axis=-1), (1, reps)),
                jnp.tile(jnp.concatenate([-sin, sin], axis=-1), (1, reps)))

    cos128, sin128 = table(DSA_HEAD_DIM, 1)
    cos64, sin64 = table(IDX_DIM, LANES // IDX_DIM)
    return cos128, sin128, cos64, sin64


def _pack_w_in(w_in):
    sizes = (GLA_QK, GLA_QK, GLA_V, GLA_GATE_RANK, GLA_V, DSA_Q, DSA_HEAD_DIM, DSA_HEAD_DIM, IDX_Q, IDX_DIM, IDX_HEADS)
    offs = np.concatenate([[0], np.cumsum(sizes)])
    g_q, g_k, g_v, g_lr, g_r, d_q, d_k, d_v, i_q, i_k, i_w = [w_in[:, offs[i]:offs[i + 1]] for i in range(len(sizes))]
    pad = jnp.zeros((w_in.shape[0], LANES - IDX_DIM - IDX_HEADS - GLA_GATE_RANK), w_in.dtype)
    return jnp.concatenate([g_q, g_k, g_v, g_r, d_q, d_k, d_v, i_q, i_k, i_w, g_lr, pad], axis=1)


def kernel(x, norm_mix_pre, w_in, gla_gate_w2, gla_gate_b, gla_norm_w, idx_k_norm_w, idx_k_norm_b,
           w_out, norm_mix_post, norm_ffn_pre, w_ff1, w_ff2, norm_ffn_post):
    batch, seq, d = x.shape
    tabs = _rope_tables(seq)
    x2 = x.reshape(batch * seq, d)
    for l in range(w_in.shape[0]):
        w_pack = _pack_w_in(w_in[l]).astype(BF16)
        w2_pad = jnp.zeros((LANES, GLA_QK), F32).at[M_LR:M_LR + GLA_GATE_RANK].set(gla_gate_w2[l]).astype(BF16)
        lnw = jnp.zeros((1, LANES), F32).at[0, :IDX_DIM].set(idx_k_norm_w[l])
        lnb = jnp.zeros((1, LANES), F32).at[0, :IDX_DIM].set(idx_k_norm_b[l])
        (gq, gk, gv, gr, glog, dq, dk, dvt, iq, ik2, misc) = _projection(
            x2, norm_mix_pre[l][None], w_pack, w2_pad, gla_gate_b[l][None], lnw, lnb, tabs, seq)
        gla = _gla(gq, gk, glog, gv, gr, gla_norm_w[l][None], batch, seq)
        dsa = _dsa(iq, misc, dq, ik2, dk, dvt, batch, seq)
        x2 = _out_ffn(gla, dsa, x2, w_out[l].astype(BF16), norm_mix_post[l][None], norm_ffn_pre[l][None],
                      w_ff1[l].astype(BF16), w_ff2[l].astype(BF16), norm_ffn_post[l][None])
    return x2.reshape(batch, seq, d)
```

```python
import functools

import jax
import jax.numpy as jnp
import numpy as np
from jax import lax
from jax.experimental import pallas as pl
from jax.experimental.pallas import tpu as pltpu

F32 = jnp.float32
BF16 = jnp.bfloat16

EPS = 1e-6
ROPE_THETA = 10000.0
CHUNK = 64
GLA_HEADS = 4
GLA_DK = 64
GLA_DV = 128
GLA_GATE_RANK = 16
GLA_TAU = 16.0
DSA_HEADS = 4
DSA_HEAD_DIM = 128
IDX_HEADS = 4
IDX_DIM = 64
TOPK_MAX = 256

GLA_QK = GLA_HEADS * GLA_DK
GLA_V = GLA_HEADS * GLA_DV
DSA_Q = DSA_HEADS * DSA_HEAD_DIM
IDX_Q = IDX_HEADS * IDX_DIM

LANES = 128
VMEM_LIMIT = 56 * 1024 * 1024

C_GQ = 0
C_GK = C_GQ + GLA_QK
C_GV = C_GK + GLA_QK
C_GR = C_GV + GLA_V
C_DQ = C_GR + GLA_V
C_DK = C_DQ + DSA_Q
C_DV = C_DK + DSA_HEAD_DIM
C_IQ = C_DV + DSA_HEAD_DIM
C_MISC = C_IQ + IDX_Q
D_PACK = C_MISC + LANES
M_IW = IDX_DIM
M_LR = IDX_DIM + IDX_HEADS

PROJ_TM = 512
GLA_TS = 1024
DSA_QB = 256
DSA_TK = 256
DSA_SEQS = 1
DSA_UNROLL = 4
DSA_UNROLL_SCORE = 4
DSA_BISECT_COARSE = 12
DSA_WALK_FIXED = 1
DSA_BISECT = 8
PACKED_ROWS = 16
BF16_STEP = 2.0 ** -7
TINY = 2.0 ** -120
DSA_ONES = PACKED_ROWS
DSA_VROWS = DSA_HEAD_DIM + DSA_ONES
DSA_Q_SCALE = DSA_HEAD_DIM ** -0.5 * 1.4426950408889634
FFN_TM = 512
FFN_TF = 1024


def _rms(x, w):
    return x * lax.rsqrt(jnp.mean(x * x, axis=-1, keepdims=True) + EPS) * w


def _proj_kernel(x_ref, nw_ref, w_ref, w2_ref, gb_ref, lnw_ref, lnb_ref,
                 cos128_ref, sin128_ref, cos64_ref, sin64_ref,
                 gq_ref, gk_ref, gv_ref, gr_ref, glog_ref,
                 dq_ref, dk_ref, dv_ref, iq_ref, ik_ref, misc_ref):
    hb = _rms(x_ref[...], nw_ref[...]).astype(BF16)

    def proj(lo, width):
        return jnp.dot(hb, w_ref[:, lo:lo + width], preferred_element_type=F32)

    cos128 = cos128_ref[...]
    sin128 = sin128_ref[...]
    cos64 = cos64_ref[...]
    sin64 = sin64_ref[...]
    lane = lax.broadcasted_iota(jnp.int32, (1, LANES), 1)
    first_half64 = (lane % IDX_DIM) < (IDX_DIM // 2)

    def rope128(t):
        return t * cos128 + pltpu.roll(t, LANES // 2, 1) * sin128

    def rope64(t):
        rot = jnp.where(first_half64, pltpu.roll(t, LANES - IDX_DIM // 2, 1), pltpu.roll(t, IDX_DIM // 2, 1))
        return t * cos64 + rot * sin64

    misc = proj(C_MISC, LANES)
    misc_ref[...] = misc * (IDX_HEADS ** -0.5 * IDX_DIM ** -0.5)

    is_ik = lane < IDX_DIM
    mu = jnp.sum(jnp.where(is_ik, misc, 0.0), axis=-1, keepdims=True) * (1.0 / IDX_DIM)
    d = jnp.where(is_ik, misc - mu, 0.0)
    var = jnp.sum(d * d, axis=-1, keepdims=True) * (1.0 / IDX_DIM)
    y = d * lax.rsqrt(var + EPS) * lnw_ref[...] + lnb_ref[...]
    yr = rope64(y)
    ik_ref[...] = (yr + pltpu.roll(yr, LANES // 2, 1)).astype(BF16)

    z = jnp.dot(misc.astype(BF16), w2_ref[...], preferred_element_type=F32) + gb_ref[...]
    glog_ref[...] = (jnp.minimum(z, 0.0) - jnp.log1p(jnp.exp(-jnp.abs(z)))) * (1.0 / GLA_TAU)

    dq = proj(C_DQ, DSA_Q)
    for h in range(DSA_HEADS):
        sl = slice(h * LANES, (h + 1) * LANES)
        dq_ref[:, sl] = (rope128(dq[:, sl]) * DSA_Q_SCALE).astype(BF16)
    kvq = proj(C_DK, 2 * DSA_HEAD_DIM + IDX_Q)
    dk_ref[...] = rope128(kvq[:, :DSA_HEAD_DIM]).astype(BF16)
    for c in range(IDX_Q // LANES):
        src = slice(C_IQ - C_DK + c * LANES, C_IQ - C_DK + (c + 1) * LANES)
        iq_ref[:, c * LANES:(c + 1) * LANES] = rope64(kvq[:, src]).astype(BF16)

    dv = kvq[:, C_DV - C_DK:C_IQ - C_DK]
    ones = jnp.ones((DSA_ONES, DSA_TK), BF16)
    for c in range(dv_ref.shape[0]):
        dv_t = dv[c * DSA_TK:(c + 1) * DSA_TK, :].T.astype(BF16)
        dv_ref[c] = jnp.concatenate([dv_t, ones], axis=0)

    gqk = proj(C_GQ, 2 * GLA_QK)
    gq_ref[...] = gqk[:, :GLA_QK]
    gk_ref[...] = gqk[:, GLA_QK:]
    gv_ref[...] = proj(C_GV, GLA_V).astype(BF16)
    gr_ref[...] = proj(C_GR, GLA_V)


def _projection(x2, nw, w_pack, w2_pad, gb, lnw, lnb, tabs, seq):
    t, d = x2.shape
    tm = PROJ_TM
    assert t % tm == 0 and seq % tm == 0
    tiles_per_seq = seq // tm

    def row(i):
        return (i, 0)

    def const(i):
        return (0, 0)

    def pos(i):
        return (i % tiles_per_seq, 0)

    widths = [(GLA_QK, F32), (GLA_QK, F32), (GLA_V, BF16), (GLA_V, F32), (GLA_QK, F32),
              (DSA_Q, BF16), (DSA_HEAD_DIM, BF16), (DSA_HEAD_DIM, BF16), (IDX_Q, BF16),
              (LANES, BF16), (LANES, F32)]
    DV_SLOT = 7
    assert tm % DSA_TK == 0
    dvt_spec = pl.BlockSpec((tm // DSA_TK, DSA_VROWS, DSA_TK), lambda i: (i, 0, 0))
    dvt_shape = jax.ShapeDtypeStruct((t // DSA_TK, DSA_VROWS, DSA_TK), BF16)
    return pl.pallas_call(
        _proj_kernel,
        grid=(t // tm,),
        in_specs=[
            pl.BlockSpec((tm, d), row),
            pl.BlockSpec((1, d), const),
            pl.BlockSpec((d, D_PACK), const),
            pl.BlockSpec((LANES, GLA_QK), const),
            pl.BlockSpec((1, GLA_QK), const),
            pl.BlockSpec((1, LANES), const),
            pl.BlockSpec((1, LANES), const),
            pl.BlockSpec((tm, LANES), pos),
            pl.BlockSpec((tm, LANES), pos),
            pl.BlockSpec((tm, LANES), pos),
            pl.BlockSpec((tm, LANES), pos),
        ],
        out_specs=[dvt_spec if n == DV_SLOT else pl.BlockSpec((tm, w), row) for n, (w, _) in enumerate(widths)],
        out_shape=[dvt_shape if n == DV_SLOT else jax.ShapeDtypeStruct((t, w), dt)
                   for n, (w, dt) in enumerate(widths)],
        compiler_params=pltpu.CompilerParams(dimension_semantics=("parallel",), vmem_limit_bytes=VMEM_LIMIT),
        name="proj",
    )(x2, nw, w_pack, w2_pad, gb, lnw, lnb, *tabs)


def _gla_kernel(q_ref, k_ref, g_ref, v_ref, r_ref, nw_ref, o_ref, st_ref):
    @pl.when(pl.program_id(1) == 0)
    def _():
        st_ref[...] = jnp.zeros_like(st_ref)

    ts = q_ref.shape[0]
    n_chunks = ts // CHUNK
    n_pairs = GLA_HEADS // 2
    ri = lax.broadcasted_iota(jnp.int32, (CHUNK, CHUNK), 0)
    ci = lax.broadcasted_iota(jnp.int32, (CHUNK, CHUNK), 1)
    causal = ri >= ci
    tri = jnp.where(causal, 1.0, 0.0).astype(BF16)
    lane = lax.broadcasted_iota(jnp.int32, (1, LANES), 1)
    half_mask = [lane < GLA_DK, lane >= GLA_DK]
    nw = nw_ref[...]
    nt = (((1,), (1,)), ((), ()))
    tn = (((0,), (0,)), ((), ()))
    chunks = [slice(c * CHUNK, (c + 1) * CHUNK) for c in range(n_chunks)]
    heads = [(p, e) for p in range(n_pairs) for e in range(2)]

    g = g_ref[...]
    g1 = g.astype(BF16)
    g2 = (g - g1.astype(F32)).astype(BF16)
    g3 = (g - g1.astype(F32) - g2.astype(F32)).astype(BF16)
    bcum = [jnp.dot(tri, g1[rows], preferred_element_type=F32)
            + jnp.dot(tri, g2[rows], preferred_element_type=F32)
            + jnp.dot(tri, g3[rows], preferred_element_type=F32) for rows in chunks]
    b_last = [b[CHUNK - 1:CHUNK, :] for b in bcum]
    decay = [jnp.exp(b) for b in b_last]
    q_dec = [q_ref[rows, :] * (GLA_DK ** -0.5) * jnp.exp(b) for rows, b in zip(chunks, bcum)]
    k_inv = [(k_ref[rows, :] * jnp.exp(-b)).astype(BF16) for rows, b in zip(chunks, bcum)]
    k_end = [k_ref[rows, :] * jnp.exp(bl - b) for rows, b, bl in zip(chunks, bcum, b_last)]

    def pair_lanes(x, p):
        return x[:, p * LANES:(p + 1) * LANES]

    qm = [[jnp.where(half_mask[e], pair_lanes(q_dec[c], p), 0.0).astype(BF16) for p, e in heads]
          for c in range(n_chunks)]
    attn = [[jnp.where(causal, lax.dot_general(qm[c][i], pair_lanes(k_inv[c], p), nt, preferred_element_type=F32),
                       0.0).astype(BF16) for i, (p, e) in enumerate(heads)] for c in range(n_chunks)]
    o_intra = [[jnp.dot(attn[c][i], v_ref[chunks[c], i * GLA_DV:(i + 1) * GLA_DV], preferred_element_type=F32)
                for i in range(GLA_HEADS)] for c in range(n_chunks)]
    upd = []
    for c in range(n_chunks):
        per_pair = []
        for p in range(n_pairs):
            ke = pair_lanes(k_end[c], p)
            ke2 = jnp.concatenate([jnp.where(half_mask[e], ke, 0.0).astype(BF16) for e in range(2)], axis=0)
            v2 = jnp.concatenate([v_ref[chunks[c], (2 * p + e) * GLA_DV:(2 * p + e + 1) * GLA_DV] for e in range(2)],
                                 axis=0)
            per_pair.append(lax.dot_general(v2, ke2, tn, preferred_element_type=F32))
        upd.append(per_pair)

    st_in = []
    for p in range(n_pairs):
        st = st_ref[p]
        per_chunk = []
        for c in range(n_chunks):
            per_chunk.append(st.astype(BF16))
            st = st * pair_lanes(decay[c], p) + upd[c][p]
        st_ref[p] = st
        st_in.append(per_chunk)

    for c in range(n_chunks):
        for i, (p, e) in enumerate(heads):
            o = o_intra[c][i] + lax.dot_general(qm[c][i], st_in[p][c], nt, preferred_element_type=F32)
            o = _rms(o, nw)
            r = r_ref[chunks[c], i * GLA_DV:(i + 1) * GLA_DV]
            o_ref[chunks[c], i * GLA_DV:(i + 1) * GLA_DV] = (o * (r * jax.nn.sigmoid(r))).astype(BF16)


def _gla(gq, gk, glog, gv, gr, nw, batch, seq):
    ts = GLA_TS
    assert seq % ts == 0 and ts % CHUNK == 0
    n = seq // ts

    def row(b, s):
        return (b * n + s, 0)

    return pl.pallas_call(
        _gla_kernel,
        grid=(batch, n),
        in_specs=[
            pl.BlockSpec((ts, GLA_QK), row),
            pl.BlockSpec((ts, GLA_QK), row),
            pl.BlockSpec((ts, GLA_QK), row),
            pl.BlockSpec((ts, GLA_V), row),
            pl.BlockSpec((ts, GLA_V), row),
            pl.BlockSpec((1, GLA_DV), lambda b, s: (0, 0)),
        ],
        out_specs=pl.BlockSpec((ts, GLA_V), row),
        out_shape=jax.ShapeDtypeStruct((batch * seq, GLA_V), BF16),
        scratch_shapes=[pltpu.VMEM((GLA_HEADS // 2, GLA_DV, 2 * GLA_DK), F32)],
        compiler_params=pltpu.CompilerParams(dimension_semantics=("parallel", "arbitrary"),
                                             vmem_limit_bytes=VMEM_LIMIT),
        name="gla",
    )(gq, gk, glog, gv, gr, nw)


def _dsa_kernel(qi_ref, misc_ref, q_ref, ik_ref, k_ref, vt_ref, o_ref,
                sc_ref, acc_ref, qm_ref, lg_ref, scb_ref, *, n_sel):
    nb, qb = qi_ref.shape[0], qi_ref.shape[1]
    width = nb * qb
    tk = sc_ref.shape[1]
    j = pl.program_id(1)
    n_tiles = ((j + 1) * qb + tk - 1) // tk
    nt = (((1,), (1,)), ((), ()))
    seq_lanes = [slice(b * qb, (b + 1) * qb) for b in range(nb)]

    col = lax.broadcasted_iota(jnp.int32, (1, width), 1) % qb
    limit = j * qb + (col // CHUNK + 1) * CHUNK
    k_eff = jnp.minimum(limit, n_sel).astype(F32)
    key_in_tile = lax.broadcasted_iota(jnp.int32, (tk, qb), 0)
    limit_q = limit[:, :qb]
    lane = lax.broadcasted_iota(jnp.int32, (1, LANES), 1)

    wh = []
    for b in range(nb):
        misc_t = misc_ref[b].T
        for h in range(IDX_HEADS):
            pair = qi_ref[b, :, (h // 2) * LANES:(h // 2 + 1) * LANES].astype(F32)
            in_head = (lane // IDX_DIM) == (h % 2)
            qm_ref[b, h] = jnp.where(in_head, pair, 0.0).astype(BF16)
        wh.append([misc_t[M_IW + h:M_IW + h + 1, :] for h in range(IDX_HEADS)])

    def for_each_tile_group(run, init, unroll):
        carry = lax.fori_loop(0, n_tiles // unroll, lambda i, c: run(unroll * i, unroll, c), init)
        first = (n_tiles // unroll) * unroll
        tail = unroll // 2
        while tail >= 1:
            carry = lax.cond((n_tiles & tail) != 0, functools.partial(run, first, tail), lambda c: c, carry)
            first = first + (n_tiles & tail)
            tail //= 2
        return carry

    def for_each_tile(body, init, unroll=2):
        def run(first, count, carry):
            for k in range(count):
                carry = body(first + k, carry)
            return carry
        return for_each_tile_group(run, init, unroll)

    def fold(op, a):
        return op(a.reshape(a.shape[0] // 8, 8, a.shape[-1]), axis=0)

    def fold_max(a):
        return fold(jnp.max, a)

    def fold_sum(a):
        return fold(jnp.sum, a)

    def key_rows(ref, b, t):
        return ref[b, pl.ds(pl.multiple_of(t * tk, tk), tk), :]

    def score_tile(t, carry):
        pmin, pmax = carry
        mins, maxs = [], []
        for b in range(nb):
            kt = key_rows(ik_ref, b, t)
            sc = None
            for h in range(IDX_HEADS):
                s = lax.dot_general(kt, qm_ref[b, h], nt, preferred_element_type=F32)
                term = wh[b][h] * jnp.maximum(s, 0.0)
                sc = term if sc is None else sc + term
            scm = jnp.where(key_in_tile < limit_q - t * tk, sc, -jnp.inf)
            sc_ref[t, :, seq_lanes[b]] = scm
            scb_ref[t, :, seq_lanes[b]] = scm.astype(BF16)
            mins.append(fold(jnp.min, sc))
            maxs.append(fold_max(scm))
        return (jnp.minimum(pmin, jnp.concatenate(mins, axis=1)),
                jnp.maximum(pmax, jnp.concatenate(maxs, axis=1)))

    pmin, pmax = for_each_tile(
        score_tile, (jnp.full((8, width), jnp.inf, F32), jnp.full((8, width), -jnp.inf, F32)), DSA_UNROLL_SCORE)
    vmin = jnp.min(pmin, axis=0, keepdims=True)
    vmax = jnp.max(pmax, axis=0, keepdims=True)

    def count_ge(thr):
        def body(t, acc):
            return acc + fold_sum(jnp.where(sc_ref[t] >= thr, 1.0, 0.0))
        acc = for_each_tile(body, jnp.zeros((8, width), F32))
        return jnp.sum(acc, axis=0, keepdims=True)

    def max_below(bound):
        def body(t, acc):
            s = sc_ref[t]
            return jnp.maximum(acc, fold_max(jnp.where(s < bound, s, -jnp.inf)))
        acc = for_each_tile(body, jnp.full((8, width), -jnp.inf, F32))
        return jnp.max(acc, axis=0, keepdims=True)

    def count_ge_coarse(thr_b):
        one, zero = jnp.ones((), BF16), jnp.zeros((), BF16)

        def body(t, acc):
            ind = jnp.where(scb_ref[t] >= thr_b, one, zero)
            parts = [ind[r * PACKED_ROWS:(r + 1) * PACKED_ROWS, :] for r in range(tk // PACKED_ROWS)]
            while len(parts) > 1:
                parts = [a + b for a, b in zip(parts[::2], parts[1::2])]
            return acc + parts[0]
        acc = for_each_tile(body, jnp.zeros((PACKED_ROWS, width), BF16))
        return jnp.sum(acc.astype(F32), axis=0, keepdims=True)

    def bisect_coarse(_, carry):
        lo, hb, hi = carry
        mid_b = (0.5 * lo + 0.5 * hb).astype(BF16)
        mid = mid_b.astype(F32)
        up = count_ge_coarse(mid_b) >= k_eff
        return jnp.where(up, mid, lo), jnp.where(up, hb, mid), jnp.where(up, hi, mid)

    lo, _, hi = lax.fori_loop(0, DSA_BISECT_COARSE, bisect_coarse,
                              (vmin, vmax, jnp.full((1, width), jnp.inf, F32)))
    lo = lo - jnp.abs(lo) * BF16_STEP - TINY
    hi = hi + jnp.abs(hi) * BF16_STEP + TINY
    hb = jnp.minimum(hi, vmax)
    c_hi = count_ge(hi)

    def bisect(_, carry):
        lo, hb, hi, c_hi = carry
        mid = 0.5 * lo + 0.5 * hb
        c = count_ge(mid)
        up = c >= k_eff
        return (jnp.where(up, mid, lo), jnp.where(up, hb, mid),
                jnp.where(up, hi, mid), jnp.where(up, c_hi, c))

    lo, hb, hi, c_hi = lax.fori_loop(0, DSA_BISECT, bisect, (lo, hb, hi, c_hi))

    max_steps = n_tiles * tk

    def walk_step(state):
        hi, c_hi, thr, done = state
        cand = max_below(hi)
        c = count_ge(cand)
        ok = jnp.logical_and(done < 0.5, c >= k_eff)
        thr = jnp.where(ok, cand, thr)
        done = jnp.where(ok, 1.0, done)
        moving = done < 0.5
        return jnp.where(moving, cand, hi), jnp.where(moving, c, c_hi), thr, done

    def remaining(state):
        return jnp.sum(jnp.where(state[3] < 0.5, 1.0, 0.0)).astype(jnp.int32)

    state = (hi, c_hi, vmin, jnp.zeros((1, width), F32))
    for _ in range(DSA_WALK_FIXED):
        state = walk_step(state)

    def not_done(carry):
        return jnp.logical_and(carry[0] > 0, carry[1] < max_steps)

    def step(carry):
        state = walk_step(carry[2])
        return remaining(state), carry[1] + 1, state

    _, _, (_, c_hi, thr, _) = lax.while_loop(not_done, step, (remaining(state), jnp.int32(0), state))
    n_tie = k_eff - c_hi

    ti = lax.broadcasted_iota(jnp.int32, (tk, tk), 0)
    tj = lax.broadcasted_iota(jnp.int32, (tk, tk), 1)
    tri_incl = jnp.where(tj <= ti, 1.0, 0.0).astype(BF16)
    neg = -1e30

    def attend(b):
        thr_b, n_tie_b = thr[:, seq_lanes[b]], n_tie[:, seq_lanes[b]]

        def logit_tile(t, carry):
            seen, pmax = carry
            s = sc_ref[t, :, seq_lanes[b]]
            tie = s == thr_b
            incl = jnp.dot(tri_incl, jnp.where(tie, 1.0, 0.0).astype(BF16), preferred_element_type=F32)
            sel = jnp.logical_or(s > thr_b, jnp.logical_and(tie, incl <= n_tie_b - seen))
            bias = jnp.where(sel, 0.0, neg)
            kt = key_rows(k_ref, b, t)
            pmax_out = []
            for h in range(DSA_HEADS):
                lg = lax.dot_general(kt, q_ref[b, :, h * LANES:(h + 1) * LANES], nt,
                                     preferred_element_type=F32) + bias
                lg_ref[t, h] = lg
                pmax_out.append(jnp.maximum(pmax[h], fold_max(lg)))
            return seen + incl[tk - 1:tk, :], tuple(pmax_out)

        pmax0 = tuple(jnp.full((8, qb), neg, F32) for _ in range(DSA_HEADS))
        _, pmax = for_each_tile(logit_tile, (jnp.zeros((1, qb), F32), pmax0), DSA_UNROLL)
        m = [jnp.max(pmax[h], axis=0, keepdims=True) for h in range(DSA_HEADS)]

        acc_ref[...] = jnp.zeros_like(acc_ref)

        def value_tiles(tiles):
            vt = jnp.concatenate([vt_ref[b, t] for t in tiles], axis=1)
            for h in range(DSA_HEADS):
                lanes = slice(h * qb, (h + 1) * qb)
                p = jnp.concatenate([jnp.exp2((lg_ref[t, h] - m[h]).astype(BF16)) for t in tiles], axis=0)
                acc_ref[:, lanes] += jnp.dot(vt, p, preferred_element_type=F32)

        @pl.loop(0, n_tiles // DSA_UNROLL)
        def _(i):
            value_tiles([DSA_UNROLL * i + k for k in range(DSA_UNROLL)])

        first = (n_tiles // DSA_UNROLL) * DSA_UNROLL
        tail = DSA_UNROLL // 2
        while tail >= 1:
            pl.when((n_tiles & tail) != 0)(functools.partial(value_tiles, [first + k for k in range(tail)]))
            first = first + (n_tiles & tail)
            tail //= 2

        for h in range(DSA_HEADS):
            lanes = slice(h * qb, (h + 1) * qb)
            out_t = acc_ref[:DSA_HEAD_DIM, lanes] / acc_ref[DSA_HEAD_DIM:DSA_HEAD_DIM + 1, lanes]
            o_ref[b, :, h * LANES:(h + 1) * LANES] = out_t.T.astype(BF16)

    for b in range(nb):
        attend(b)


def _dsa(iq, misc, dq, ik2, dk, dvt, batch, seq):
    qb, tk = DSA_QB, DSA_TK
    assert seq % qb == 0 and seq % tk == 0 and qb % CHUNK == 0
    nq = seq // qb
    nkt = seq // tk
    n_sel = min(TOPK_MAX, seq // 4)

    nb = DSA_SEQS if batch % DSA_SEQS == 0 else 1

    def per_seq(a):
        return a.reshape((batch, seq) + a.shape[1:])

    def qrow(b, j):
        return (b, j, 0)

    def kv(b, j):
        return (b, 0, 0)

    out = pl.pallas_call(
        functools.partial(_dsa_kernel, n_sel=n_sel),
        grid=(batch // nb, nq),
        in_specs=[
            pl.BlockSpec((nb, qb, IDX_Q), qrow),
            pl.BlockSpec((nb, qb, LANES), qrow),
            pl.BlockSpec((nb, qb, DSA_Q), qrow),
            pl.BlockSpec((nb, seq, LANES), kv),
            pl.BlockSpec((nb, seq, DSA_HEAD_DIM), kv),
            pl.BlockSpec((nb, nkt, DSA_VROWS, tk), lambda b, j: (b, 0, 0, 0)),
        ],
        out_specs=pl.BlockSpec((nb, qb, DSA_Q), qrow),
        out_shape=jax.ShapeDtypeStruct((batch, seq, DSA_Q), BF16),
        scratch_shapes=[pltpu.VMEM((nkt, tk, nb * qb), F32),
                        pltpu.VMEM((DSA_VROWS, DSA_HEADS * qb), F32),
                        pltpu.VMEM((nb, IDX_HEADS, qb, LANES), BF16),
                        pltpu.VMEM((nkt, DSA_HEADS, tk, qb), F32),
                        pltpu.VMEM((nkt, tk, nb * qb), BF16)],
        compiler_params=pltpu.CompilerParams(dimension_semantics=("parallel", "arbitrary"),
                                             vmem_limit_bytes=VMEM_LIMIT),
        name="dsa",
    )(per_seq(iq), per_seq(misc), per_seq(dq), per_seq(ik2), per_seq(dk),
      dvt.reshape((batch, nkt) + dvt.shape[1:]))
    return out.reshape(batch * seq, DSA_Q)


def _out_ffn_kernel(gla_ref, dsa_ref, x_ref, wo_ref, n1_ref, n2_ref, w1_ref, w2_ref, n3_ref, o_ref):
    mixed = (jnp.dot(gla_ref[...], wo_ref[:GLA_V, :], preferred_element_type=F32)
             + jnp.dot(dsa_ref[...], wo_ref[GLA_V:, :], preferred_element_type=F32))
    x1 = x_ref[...] + _rms(mixed, n1_ref[...])
    h = _rms(x1, n2_ref[...]).astype(BF16)
    d_ff = w1_ref.shape[1]
    acc = jnp.zeros(x1.shape, F32)
    for f in range(d_ff // FFN_TF):
        cols = slice(f * FFN_TF, (f + 1) * FFN_TF)
        a = jnp.maximum(jnp.dot(h, w1_ref[:, cols], preferred_element_type=F32), 0.0)
        acc = acc + jnp.dot((a * a).astype(BF16), w2_ref[cols, :], preferred_element_type=F32)
    o_ref[...] = x1 + _rms(acc, n3_ref[...])


def _out_ffn(gla, dsa, x2, wo, n1, n2, w1, w2, n3):
    t, d = x2.shape
    tm = FFN_TM
    d_ff = w1.shape[1]
    assert t % tm == 0 and d_ff % FFN_TF == 0

    def row(i):
        return (i, 0)

    def const(i):
        return (0, 0)

    def resident(shape):
        return pl.BlockSpec(shape, const, pipeline_mode=pl.Buffered(1))

    return pl.pallas_call(
        _out_ffn_kernel,
        grid=(t // tm,),
        in_specs=[
            pl.BlockSpec((tm, GLA_V), row),
            pl.BlockSpec((tm, DSA_Q), row),
            pl.BlockSpec((tm, d), row),
            resident((GLA_V + DSA_Q, d)),
            resident((1, d)),
            resident((1, d)),
            resident((d, d_ff)),
            resident((d_ff, d)),
            resident((1, d)),
        ],
        out_specs=pl.BlockSpec((tm, d), row),
        out_shape=jax.ShapeDtypeStruct((t, d), F32),
        compiler_params=pltpu.CompilerParams(dimension_semantics=("parallel",), vmem_limit_bytes=VMEM_LIMIT),
        name="out_ffn",
    )(gla, dsa, x2, wo, n1, n2, w1, w2, n3)


def _rope_tables(seq):
    pos = jnp.arange(seq, dtype=F32)[:, None]

    def table(dim, reps):
        inv = ROPE_THETA ** (-jnp.arange(0, dim, 2, dtype=F32) / dim)
        ang = pos * inv[None, :]
        cos, sin = jnp.cos(ang), jnp.sin(ang)
        return (jnp.tile(jnp.concatenate([cos, cos], axis=-1), (1, reps)),
                jnp.tile(jnp.concatenate([-sin, sin], axis=-1), (1, reps)))

    cos128, sin128 = table(DSA_HEAD_DIM, 1)
    cos64, sin64 = table(IDX_DIM, LANES // IDX_DIM)
    return cos128, sin128, cos64, sin64


def _pack_w_in(w_in):
    sizes = (GLA_QK, GLA_QK, GLA_V, GLA_GATE_RANK, GLA_V, DSA_Q, DSA_HEAD_DIM, DSA_HEAD_DIM, IDX_Q, IDX_DIM, IDX_HEADS)
    offs = np.concatenate([[0], np.cumsum(sizes)])
    g_q, g_k, g_v, g_lr, g_r, d_q, d_k, d_v, i_q, i_k, i_w = [w_in[:, offs[i]:offs[i + 1]] for i in range(len(sizes))]
    pad = jnp.zeros((w_in.shape[0], LANES - IDX_DIM - IDX_HEADS - GLA_GATE_RANK), w_in.dtype)
    return jnp.concatenate([g_q, g_k, g_v, g_r, d_q, d_k, d_v, i_q, i_k, i_w, g_lr, pad], axis=1)


def kernel(x, norm_mix_pre, w_in, gla_gate_w2, gla_gate_b, gla_norm_w, idx_k_norm_w, idx_k_norm_b,
           w_out, norm_mix_post, norm_ffn_pre, w_ff1, w_ff2, norm_ffn_post):
    batch, seq, d = x.shape
    tabs = _rope_tables(seq)
    x2 = x.reshape(batch * seq, d)
    for l in range(w_in.shape[0]):
        w_pack = _pack_w_in(w_in[l]).astype(BF16)
        w2_pad = jnp.zeros((LANES, GLA_QK), F32).at[M_LR:M_LR + GLA_GATE_RANK].set(gla_gate_w2[l]).astype(BF16)
        lnw = jnp.zeros((1, LANES), F32).at[0, :IDX_DIM].set(idx_k_norm_w[l])
        lnb = jnp.zeros((1, LANES), F32).at[0, :IDX_DIM].set(idx_k_norm_b[l])
        (gq, gk, gv, gr, glog, dq, dk, dvt, iq, ik2, misc) = _projection(
            x2, norm_mix_pre[l][None], w_pack, w2_pad, gla_gate_b[l][None], lnw, lnb, tabs, seq)
        gla = _gla(gq, gk, glog, gv, gr, gla_norm_w[l][None], batch, seq)
        dsa = _dsa(iq, misc, dq, ik2, dk, dvt, batch, seq)
        x2 = _out_ffn(gla, dsa, x2, w_out[l].astype(BF16), norm_mix_post[l][None], norm_ffn_pre[l][None],
                      w_ff1[l].astype(BF16), w_ff2[l].astype(BF16), norm_ffn_post[l][None])
    return x2.reshape(batch, seq, d)
```

```python
import functools

import jax
import jax.numpy as jnp
import numpy as np
from jax import lax
from jax.experimental import pallas as pl
from jax.experimental.pallas import tpu as pltpu

F32 = jnp.float32
BF16 = jnp.bfloat16

EPS = 1e-6
ROPE_THETA = 10000.0
CHUNK = 64
GLA_HEADS = 4
GLA_DK = 64
GLA_DV = 128
GLA_GATE_RANK = 16
GLA_TAU = 16.0
DSA_HEADS = 4
DSA_HEAD_DIM = 128
IDX_HEADS = 4
IDX_DIM = 64
TOPK_MAX = 256

GLA_QK = GLA_HEADS * GLA_DK
GLA_V = GLA_HEADS * GLA_DV
DSA_Q = DSA_HEADS * DSA_HEAD_DIM
IDX_Q = IDX_HEADS * IDX_DIM

LANES = 128
VMEM_LIMIT = 56 * 1024 * 1024

C_GQ = 0
C_GK = C_GQ + GLA_QK
C_GV = C_GK + GLA_QK
C_GR = C_GV + GLA_V
C_DQ = C_GR + GLA_V
C_DK = C_DQ + DSA_Q
C_DV = C_DK + DSA_HEAD_DIM
C_IQ = C_DV + DSA_HEAD_DIM
C_MISC = C_IQ + IDX_Q
D_PACK = C_MISC + LANES
M_IW = IDX_DIM
M_LR = IDX_DIM + IDX_HEADS

PROJ_TM = 512
GLA_TS = 1024
DSA_QB = 256
DSA_TK = 256
DSA_SEQS = 1
DSA_UNROLL = 8
DSA_UNROLL_SCORE = 8
DSA_BISECT_COARSE = 12
DSA_WALK_FIXED = 1
DSA_BISECT = 8
PACKED_ROWS = 16
BF16_STEP = 2.0 ** -7
TINY = 2.0 ** -120
DSA_ONES = PACKED_ROWS
DSA_VROWS = DSA_HEAD_DIM + DSA_ONES
DSA_Q_SCALE = DSA_HEAD_DIM ** -0.5 * 1.4426950408889634
FFN_TM = 512
FFN_TF = 1024


def _rms(x, w):
    return x * lax.rsqrt(jnp.mean(x * x, axis=-1, keepdims=True) + EPS) * w


def _proj_kernel(x_ref, nw_ref, w_ref, w2_ref, gb_ref, lnw_ref, lnb_ref,
                 cos128_ref, sin128_ref, cos64_ref, sin64_ref,
                 gq_ref, gk_ref, gv_ref, gr_ref, glog_ref,
                 dq_ref, dk_ref, dv_ref, iq_ref, ik_ref, misc_ref):
    hb = _rms(x_ref[...], nw_ref[...]).astype(BF16)

    def proj(lo, width):
        return jnp.dot(hb, w_ref[:, lo:lo + width], preferred_element_type=F32)

    cos128 = cos128_ref[...]
    sin128 = sin128_ref[...]
    cos64 = cos64_ref[...]
    sin64 = sin64_ref[...]
    lane = lax.broadcasted_iota(jnp.int32, (1, LANES), 1)
    first_half64 = (lane % IDX_DIM) < (IDX_DIM // 2)

    def rope128(t):
        return t * cos128 + pltpu.roll(t, LANES // 2, 1) * sin128

    def rope64(t):
        rot = jnp.where(first_half64, pltpu.roll(t, LANES - IDX_DIM // 2, 1), pltpu.roll(t, IDX_DIM // 2, 1))
        return t * cos64 + rot * sin64

    misc = proj(C_MISC, LANES)
    misc_ref[...] = misc * (IDX_HEADS ** -0.5 * IDX_DIM ** -0.5)

    is_ik = lane < IDX_DIM
    mu = jnp.sum(jnp.where(is_ik, misc, 0.0), axis=-1, keepdims=True) * (1.0 / IDX_DIM)
    d = jnp.where(is_ik, misc - mu, 0.0)
    var = jnp.sum(d * d, axis=-1, keepdims=True) * (1.0 / IDX_DIM)
    y = d * lax.rsqrt(var + EPS) * lnw_ref[...] + lnb_ref[...]
    yr = rope64(y)
    ik_ref[...] = (yr + pltpu.roll(yr, LANES // 2, 1)).astype(BF16)

    z = jnp.dot(misc.astype(BF16), w2_ref[...], preferred_element_type=F32) + gb_ref[...]
    glog_ref[...] = (jnp.minimum(z, 0.0) - jnp.log1p(jnp.exp(-jnp.abs(z)))) * (1.0 / GLA_TAU)

    dq = proj(C_DQ, DSA_Q)
    for h in range(DSA_HEADS):
        sl = slice(h * LANES, (h + 1) * LANES)
        dq_ref[:, sl] = (rope128(dq[:, sl]) * DSA_Q_SCALE).astype(BF16)
    kvq = proj(C_DK, 2 * DSA_HEAD_DIM + IDX_Q)
    dk_ref[...] = rope128(kvq[:, :DSA_HEAD_DIM]).astype(BF16)
    for c in range(IDX_Q // LANES):
        src = slice(C_IQ - C_DK + c * LANES, C_IQ - C_DK + (c + 1) * LANES)
        iq_ref[:, c * LANES:(c + 1) * LANES] = rope64(kvq[:, src]).astype(BF16)

    dv = kvq[:, C_DV - C_DK:C_IQ - C_DK]
    ones = jnp.ones((DSA_ONES, DSA_TK), BF16)
    for c in range(dv_ref.shape[0]):
        dv_t = dv[c * DSA_TK:(c + 1) * DSA_TK, :].T.astype(BF16)
        dv_ref[c] = jnp.concatenate([dv_t, ones], axis=0)

    gqk = proj(C_GQ, 2 * GLA_QK)
    gq_ref[...] = gqk[:, :GLA_QK]
    gk_ref[...] = gqk[:, GLA_QK:]
    gv_ref[...] = proj(C_GV, GLA_V).astype(BF16)
    gr_ref[...] = proj(C_GR, GLA_V)


def _projection(x2, nw, w_pack, w2_pad, gb, lnw, lnb, tabs, seq):
    t, d = x2.shape
    tm = PROJ_TM
    assert t % tm == 0 and seq % tm == 0
    tiles_per_seq = seq // tm

    def row(i):
        return (i, 0)

    def const(i):
        return (0, 0)

    def pos(i):
        return (i % tiles_per_seq, 0)

    widths = [(GLA_QK, F32), (GLA_QK, F32), (GLA_V, BF16), (GLA_V, F32), (GLA_QK, F32),
              (DSA_Q, BF16), (DSA_HEAD_DIM, BF16), (DSA_HEAD_DIM, BF16), (IDX_Q, BF16),
              (LANES, BF16), (LANES, F32)]
    DV_SLOT = 7
    assert tm % DSA_TK == 0
    dvt_spec = pl.BlockSpec((tm // DSA_TK, DSA_VROWS, DSA_TK), lambda i: (i, 0, 0))
    dvt_shape = jax.ShapeDtypeStruct((t // DSA_TK, DSA_VROWS, DSA_TK), BF16)
    return pl.pallas_call(
        _proj_kernel,
        grid=(t // tm,),
        in_specs=[
            pl.BlockSpec((tm, d), row),
            pl.BlockSpec((1, d), const),
            pl.BlockSpec((d, D_PACK), const),
            pl.BlockSpec((LANES, GLA_QK), const),
            pl.BlockSpec((1, GLA_QK), const),
            pl.BlockSpec((1, LANES), const),
            pl.BlockSpec((1, LANES), const),
            pl.BlockSpec((tm, LANES), pos),
            pl.BlockSpec((tm, LANES), pos),
            pl.BlockSpec((tm, LANES), pos),
            pl.BlockSpec((tm, LANES), pos),
        ],
        out_specs=[dvt_spec if n == DV_SLOT else pl.BlockSpec((tm, w), row) for n, (w, _) in enumerate(widths)],
        out_shape=[dvt_shape if n == DV_SLOT else jax.ShapeDtypeStruct((t, w), dt)
                   for n, (w, dt) in enumerate(widths)],
        compiler_params=pltpu.CompilerParams(dimension_semantics=("parallel",), vmem_limit_bytes=VMEM_LIMIT),
        name="proj",
    )(x2, nw, w_pack, w2_pad, gb, lnw, lnb, *tabs)


def _gla_kernel(q_ref, k_ref, g_ref, v_ref, r_ref, nw_ref, o_ref, st_ref):
    @pl.when(pl.program_id(1) == 0)
    def _():
        st_ref[...] = jnp.zeros_like(st_ref)

    ts = q_ref.shape[0]
    n_chunks = ts // CHUNK
    n_pairs = GLA_HEADS // 2
    ri = lax.broadcasted_iota(jnp.int32, (CHUNK, CHUNK), 0)
    ci = lax.broadcasted_iota(jnp.int32, (CHUNK, CHUNK), 1)
    causal = ri >= ci
    tri = jnp.where(causal, 1.0, 0.0).astype(BF16)
    lane = lax.broadcasted_iota(jnp.int32, (1, LANES), 1)
    half_mask = [lane < GLA_DK, lane >= GLA_DK]
    nw = nw_ref[...]
    nt = (((1,), (1,)), ((), ()))
    tn = (((0,), (0,)), ((), ()))
    chunks = [slice(c * CHUNK, (c + 1) * CHUNK) for c in range(n_chunks)]
    heads = [(p, e) for p in range(n_pairs) for e in range(2)]

    g = g_ref[...]
    g1 = g.astype(BF16)
    g2 = (g - g1.astype(F32)).astype(BF16)
    g3 = (g - g1.astype(F32) - g2.astype(F32)).astype(BF16)
    bcum = [jnp.dot(tri, g1[rows], preferred_element_type=F32)
            + jnp.dot(tri, g2[rows], preferred_element_type=F32)
            + jnp.dot(tri, g3[rows], preferred_element_type=F32) for rows in chunks]
    b_last = [b[CHUNK - 1:CHUNK, :] for b in bcum]
    decay = [jnp.exp(b) for b in b_last]
    q_dec = [q_ref[rows, :] * (GLA_DK ** -0.5) * jnp.exp(b) for rows, b in zip(chunks, bcum)]
    k_inv = [(k_ref[rows, :] * jnp.exp(-b)).astype(BF16) for rows, b in zip(chunks, bcum)]
    k_end = [k_ref[rows, :] * jnp.exp(bl - b) for rows, b, bl in zip(chunks, bcum, b_last)]

    def pair_lanes(x, p):
        return x[:, p * LANES:(p + 1) * LANES]

    qm = [[jnp.where(half_mask[e], pair_lanes(q_dec[c], p), 0.0).astype(BF16) for p, e in heads]
          for c in range(n_chunks)]
    attn = [[jnp.where(causal, lax.dot_general(qm[c][i], pair_lanes(k_inv[c], p), nt, preferred_element_type=F32),
                       0.0).astype(BF16) for i, (p, e) in enumerate(heads)] for c in range(n_chunks)]
    o_intra = [[jnp.dot(attn[c][i], v_ref[chunks[c], i * GLA_DV:(i + 1) * GLA_DV], preferred_element_type=F32)
                for i in range(GLA_HEADS)] for c in range(n_chunks)]
    upd = []
    for c in range(n_chunks):
        per_pair = []
        for p in range(n_pairs):
            ke = pair_lanes(k_end[c], p)
            ke2 = jnp.concatenate([jnp.where(half_mask[e], ke, 0.0).astype(BF16) for e in range(2)], axis=0)
            v2 = jnp.concatenate([v_ref[chunks[c], (2 * p + e) * GLA_DV:(2 * p + e + 1) * GLA_DV] for e in range(2)],
                                 axis=0)
            per_pair.append(lax.dot_general(v2, ke2, tn, preferred_element_type=F32))
        upd.append(per_pair)

    st_in = []
    for p in range(n_pairs):
        st = st_ref[p]
        per_chunk = []
        for c in range(n_chunks):
            per_chunk.append(st.astype(BF16))
            st = st * pair_lanes(decay[c], p) + upd[c][p]
        st_ref[p] = st
        st_in.append(per_chunk)

    for c in range(n_chunks):
        for i, (p, e) in enumerate(heads):
            o = o_intra[c][i] + lax.dot_general(qm[c][i], st_in[p][c], nt, preferred_element_type=F32)
            o = _rms(o, nw)
            r = r_ref[chunks[c], i * GLA_DV:(i + 1) * GLA_DV]
            o_ref[chunks[c], i * GLA_DV:(i + 1) * GLA_DV] = (o * (r * jax.nn.sigmoid(r))).astype(BF16)


def _gla(gq, gk, glog, gv, gr, nw, batch, seq):
    ts = GLA_TS
    assert seq % ts == 0 and ts % CHUNK == 0
    n = seq // ts

    def row(b, s):
        return (b * n + s, 0)

    return pl.pallas_call(
        _gla_kernel,
        grid=(batch, n),
        in_specs=[
            pl.BlockSpec((ts, GLA_QK), row),
            pl.BlockSpec((ts, GLA_QK), row),
            pl.BlockSpec((ts, GLA_QK), row),
            pl.BlockSpec((ts, GLA_V), row),
            pl.BlockSpec((ts, GLA_V), row),
            pl.BlockSpec((1, GLA_DV), lambda b, s: (0, 0)),
        ],
        out_specs=pl.BlockSpec((ts, GLA_V), row),
        out_shape=jax.ShapeDtypeStruct((batch * seq, GLA_V), BF16),
        scratch_shapes=[pltpu.VMEM((GLA_HEADS // 2, GLA_DV, 2 * GLA_DK), F32)],
        compiler_params=pltpu.CompilerParams(dimension_semantics=("parallel", "arbitrary"),
                                             vmem_limit_bytes=VMEM_LIMIT),
        name="gla",
    )(gq, gk, glog, gv, gr, nw)


def _dsa_kernel(qi_ref, misc_ref, q_ref, ik_ref, k_ref, vt_ref, o_ref,
                sc_ref, acc_ref, qm_ref, lg_ref, scb_ref, *, n_sel):
    nb, qb = qi_ref.shape[0], qi_ref.shape[1]
    width = nb * qb
    tk = sc_ref.shape[1]
    j = pl.program_id(1)
    n_tiles = ((j + 1) * qb + tk - 1) // tk
    nt = (((1,), (1,)), ((), ()))
    seq_lanes = [slice(b * qb, (b + 1) * qb) for b in range(nb)]

    col = lax.broadcasted_iota(jnp.int32, (1, width), 1) % qb
    limit = j * qb + (col // CHUNK + 1) * CHUNK
    k_eff = jnp.minimum(limit, n_sel).astype(F32)
    key_in_tile = lax.broadcasted_iota(jnp.int32, (tk, qb), 0)
    limit_q = limit[:, :qb]
    lane = lax.broadcasted_iota(jnp.int32, (1, LANES), 1)

    wh = []
    for b in range(nb):
        misc_t = misc_ref[b].T
        for h in range(IDX_HEADS):
            pair = qi_ref[b, :, (h // 2) * LANES:(h // 2 + 1) * LANES].astype(F32)
            in_head = (lane // IDX_DIM) == (h % 2)
            qm_ref[b, h] = jnp.where(in_head, pair, 0.0).astype(BF16)
        wh.append([misc_t[M_IW + h:M_IW + h + 1, :] for h in range(IDX_HEADS)])

    def for_each_tile_group(run, init, unroll):
        carry = lax.fori_loop(0, n_tiles // unroll, lambda i, c: run(unroll * i, unroll, c), init)
        first = (n_tiles // unroll) * unroll
        tail = unroll // 2
        while tail >= 1:
            carry = lax.cond((n_tiles & tail) != 0, functools.partial(run, first, tail), lambda c: c, carry)
            first = first + (n_tiles & tail)
            tail //= 2
        return carry

    def for_each_tile(body, init, unroll=2):
        def run(first, count, carry):
            for k in range(count):
                carry = body(first + k, carry)
            return carry
        return for_each_tile_group(run, init, unroll)

    def fold(op, a):
        return op(a.reshape(a.shape[0] // 8, 8, a.shape[-1]), axis=0)

    def fold_max(a):
        return fold(jnp.max, a)

    def fold_sum(a):
        return fold(jnp.sum, a)

    def key_rows(ref, b, t):
        return ref[b, pl.ds(pl.multiple_of(t * tk, tk), tk), :]

    def score_tile(t, carry):
        pmin, pmax = carry
        mins, maxs = [], []
        for b in range(nb):
            kt = key_rows(ik_ref, b, t)
            sc = None
            for h in range(IDX_HEADS):
                s = lax.dot_general(kt, qm_ref[b, h], nt, preferred_element_type=F32)
                term = wh[b][h] * jnp.maximum(s, 0.0)
                sc = term if sc is None else sc + term
            scm = jnp.where(key_in_tile < limit_q - t * tk, sc, -jnp.inf)
            sc_ref[t, :, seq_lanes[b]] = scm
            scb_ref[t, :, seq_lanes[b]] = scm.astype(BF16)
            mins.append(fold(jnp.min, sc))
            maxs.append(fold_max(scm))
        return (jnp.minimum(pmin, jnp.concatenate(mins, axis=1)),
                jnp.maximum(pmax, jnp.concatenate(maxs, axis=1)))

    pmin, pmax = for_each_tile(
        score_tile, (jnp.full((8, width), jnp.inf, F32), jnp.full((8, width), -jnp.inf, F32)), DSA_UNROLL_SCORE)
    vmin = jnp.min(pmin, axis=0, keepdims=True)
    vmax = jnp.max(pmax, axis=0, keepdims=True)

    def count_ge(thr):
        def body(t, acc):
            return acc + fold_sum(jnp.where(sc_ref[t] >= thr, 1.0, 0.0))
        acc = for_each_tile(body, jnp.zeros((8, width), F32))
        return jnp.sum(acc, axis=0, keepdims=True)

    def max_below(bound):
        def body(t, acc):
            s = sc_ref[t]
            return jnp.maximum(acc, fold_max(jnp.where(s < bound, s, -jnp.inf)))
        acc = for_each_tile(body, jnp.full((8, width), -jnp.inf, F32))
        return jnp.max(acc, axis=0, keepdims=True)

    def count_ge_coarse(thr_b):
        one, zero = jnp.ones((), BF16), jnp.zeros((), BF16)

        def body(t, acc):
            ind = jnp.where(scb_ref[t] >= thr_b, one, zero)
            parts = [ind[r * PACKED_ROWS:(r + 1) * PACKED_ROWS, :] for r in range(tk // PACKED_ROWS)]
            while len(parts) > 1:
                parts = [a + b for a, b in zip(parts[::2], parts[1::2])]
            return acc + parts[0]
        acc = for_each_tile(body, jnp.zeros((PACKED_ROWS, width), BF16))
        return jnp.sum(acc.astype(F32), axis=0, keepdims=True)

    def bisect_coarse(_, carry):
        lo, hb, hi = carry
        mid_b = (0.5 * lo + 0.5 * hb).astype(BF16)
        mid = mid_b.astype(F32)
        up = count_ge_coarse(mid_b) >= k_eff
        return jnp.where(up, mid, lo), jnp.where(up, hb, mid), jnp.where(up, hi, mid)

    lo, _, hi = lax.fori_loop(0, DSA_BISECT_COARSE, bisect_coarse,
                              (vmin, vmax, jnp.full((1, width), jnp.inf, F32)))
    lo = lo - jnp.abs(lo) * BF16_STEP - TINY
    hi = hi + jnp.abs(hi) * BF16_STEP + TINY
    hb = jnp.minimum(hi, vmax)
    c_hi = count_ge(hi)

    def bisect(_, carry):
        lo, hb, hi, c_hi = carry
        mid = 0.5 * lo + 0.5 * hb
        c = count_ge(mid)
        up = c >= k_eff
        return (jnp.where(up, mid, lo), jnp.where(up, hb, mid),
                jnp.where(up, hi, mid), jnp.where(up, c_hi, c))

    lo, hb, hi, c_hi = lax.fori_loop(0, DSA_BISECT, bisect, (lo, hb, hi, c_hi))

    max_steps = n_tiles * tk

    def walk_step(state):
        hi, c_hi, thr, done = state
        cand = max_below(hi)
        c = count_ge(cand)
        ok = jnp.logical_and(done < 0.5, c >= k_eff)
        thr = jnp.where(ok, cand, thr)
        done = jnp.where(ok, 1.0, done)
        moving = done < 0.5
        return jnp.where(moving, cand, hi), jnp.where(moving, c, c_hi), thr, done

    def remaining(state):
        return jnp.sum(jnp.where(state[3] < 0.5, 1.0, 0.0)).astype(jnp.int32)

    state = (hi, c_hi, vmin, jnp.zeros((1, width), F32))
    for _ in range(DSA_WALK_FIXED):
        state = walk_step(state)

    def not_done(carry):
        return jnp.logical_and(carry[0] > 0, carry[1] < max_steps)

    def step(carry):
        state = walk_step(carry[2])
        return remaining(state), carry[1] + 1, state

    _, _, (_, c_hi, thr, _) = lax.while_loop(not_done, step, (remaining(state), jnp.int32(0), state))
    n_tie = k_eff - c_hi

    ti = lax.broadcasted_iota(jnp.int32, (tk, tk), 0)
    tj = lax.broadcasted_iota(jnp.int32, (tk, tk), 1)
    tri_incl = jnp.where(tj <= ti, 1.0, 0.0).astype(BF16)
    neg = -1e30

    def attend(b):
        thr_b, n_tie_b = thr[:, seq_lanes[b]], n_tie[:, seq_lanes[b]]

        def logit_tile(t, carry):
            seen, pmax = carry
            s = sc_ref[t, :, seq_lanes[b]]
            tie = s == thr_b
            incl = jnp.dot(tri_incl, jnp.where(tie, 1.0, 0.0).astype(BF16), preferred_element_type=F32)
            sel = jnp.logical_or(s > thr_b, jnp.logical_and(tie, incl <= n_tie_b - seen))
            bias = jnp.where(sel, 0.0, neg)
            kt = key_rows(k_ref, b, t)
            pmax_out = []
            for h in range(DSA_HEADS):
                lg = lax.dot_general(kt, q_ref[b, :, h * LANES:(h + 1) * LANES], nt,
                                     preferred_element_type=F32) + bias
                lg_ref[t, h] = lg
                pmax_out.append(jnp.maximum(pmax[h], fold_max(lg)))
            return seen + incl[tk - 1:tk, :], tuple(pmax_out)

        pmax0 = tuple(jnp.full((8, qb), neg, F32) for _ in range(DSA_HEADS))
        _, pmax = for_each_tile(logit_tile, (jnp.zeros((1, qb), F32), pmax0), DSA_UNROLL)
        m = [jnp.max(pmax[h], axis=0, keepdims=True) for h in range(DSA_HEADS)]

        acc_ref[...] = jnp.zeros_like(acc_ref)

        def value_tiles(tiles):
            vt = jnp.concatenate([vt_ref[b, t] for t in tiles], axis=1)
            for h in range(DSA_HEADS):
                lanes = slice(h * qb, (h + 1) * qb)
                p = jnp.concatenate([jnp.exp2((lg_ref[t, h] - m[h]).astype(BF16)) for t in tiles], axis=0)
                acc_ref[:, lanes] += jnp.dot(vt, p, preferred_element_type=F32)

        @pl.loop(0, n_tiles // DSA_UNROLL)
        def _(i):
            value_tiles([DSA_UNROLL * i + k for k in range(DSA_UNROLL)])

        first = (n_tiles // DSA_UNROLL) * DSA_UNROLL
        tail = DSA_UNROLL // 2
        while tail >= 1:
            pl.when((n_tiles & tail) != 0)(functools.partial(value_tiles, [first + k for k in range(tail)]))
            first = first + (n_tiles & tail)
            tail //= 2

        for h in range(DSA_HEADS):
            lanes = slice(h * qb, (h + 1) * qb)
            out_t = acc_ref[:DSA_HEAD_DIM, lanes] / acc_ref[DSA_HEAD_DIM:DSA_HEAD_DIM + 1, lanes]
            o_ref[b, :, h * LANES:(h + 1) * LANES] = out_t.T.astype(BF16)

    for b in range(nb):
        attend(b)


def _dsa(iq, misc, dq, ik2, dk, dvt, batch, seq):
    qb, tk = DSA_QB, DSA_TK
    assert seq % qb == 0 and seq % tk == 0 and qb % CHUNK == 0
    nq = seq // qb
    nkt = seq // tk
    n_sel = min(TOPK_MAX, seq // 4)

    nb = DSA_SEQS if batch % DSA_SEQS == 0 else 1

    def per_seq(a):
        return a.reshape((batch, seq) + a.shape[1:])

    def qrow(b, j):
        return (b, j, 0)

    def kv(b, j):
        return (b, 0, 0)

    out = pl.pallas_call(
        functools.partial(_dsa_kernel, n_sel=n_sel),
        grid=(batch // nb, nq),
        in_specs=[
            pl.BlockSpec((nb, qb, IDX_Q), qrow),
            pl.BlockSpec((nb, qb, LANES), qrow),
            pl.BlockSpec((nb, qb, DSA_Q), qrow),
            pl.BlockSpec((nb, seq, LANES), kv),
            pl.BlockSpec((nb, seq, DSA_HEAD_DIM), kv),
            pl.BlockSpec((nb, nkt, DSA_VROWS, tk), lambda b, j: (b, 0, 0, 0)),
        ],
        out_specs=pl.BlockSpec((nb, qb, DSA_Q), qrow),
        out_shape=jax.ShapeDtypeStruct((batch, seq, DSA_Q), BF16),
        scratch_shapes=[pltpu.VMEM((nkt, tk, nb * qb), F32),
                        pltpu.VMEM((DSA_VROWS, DSA_HEADS * qb), F32),
                        pltpu.VMEM((nb, IDX_HEADS, qb, LANES), BF16),
                        pltpu.VMEM((nkt, DSA_HEADS, tk, qb), F32),
                        pltpu.VMEM((nkt, tk, nb * qb), BF16)],
        compiler_params=pltpu.CompilerParams(dimension_semantics=("parallel", "arbitrary"),
                                             vmem_limit_bytes=VMEM_LIMIT),
        name="dsa",
    )(per_seq(iq), per_seq(misc), per_seq(dq), per_seq(ik2), per_seq(dk),
      dvt.reshape((batch, nkt) + dvt.shape[1:]))
    return out.reshape(batch * seq, DSA_Q)


def _out_ffn_kernel(gla_ref, dsa_ref, x_ref, wo_ref, n1_ref, n2_ref, w1_ref, w2_ref, n3_ref, o_ref):
    mixed = (jnp.dot(gla_ref[...], wo_ref[:GLA_V, :], preferred_element_type=F32)
             + jnp.dot(dsa_ref[...], wo_ref[GLA_V:, :], preferred_element_type=F32))
    x1 = x_ref[...] + _rms(mixed, n1_ref[...])
    h = _rms(x1, n2_ref[...]).astype(BF16)
    d_ff = w1_ref.shape[1]
    acc = jnp.zeros(x1.shape, F32)
    for f in range(d_ff // FFN_TF):
        cols = slice(f * FFN_TF, (f + 1) * FFN_TF)
        a = jnp.maximum(jnp.dot(h, w1_ref[:, cols], preferred_element_type=F32), 0.0)
        acc = acc + jnp.dot((a * a).astype(BF16), w2_ref[cols, :], preferred_element_type=F32)
    o_ref[...] = x1 + _rms(acc, n3_ref[...])


def _out_ffn(gla, dsa, x2, wo, n1, n2, w1, w2, n3):
    t, d = x2.shape
    tm = FFN_TM
    d_ff = w1.shape[1]
    assert t % tm == 0 and d_ff % FFN_TF == 0

    def row(i):
        return (i, 0)

    def const(i):
        return (0, 0)

    def resident(shape):
        return pl.BlockSpec(shape, const, pipeline_mode=pl.Buffered(1))

    return pl.pallas_call(
        _out_ffn_kernel,
        grid=(t // tm,),
        in_specs=[
            pl.BlockSpec((tm, GLA_V), row),
            pl.BlockSpec((tm, DSA_Q), row),
            pl.BlockSpec((tm, d), row),
            resident((GLA_V + DSA_Q, d)),
            resident((1, d)),
            resident((1, d)),
            resident((d, d_ff)),
            resident((d_ff, d)),
            resident((1, d)),
        ],
        out_specs=pl.BlockSpec((tm, d), row),
        out_shape=jax.ShapeDtypeStruct((t, d), F32),
        compiler_params=pltpu.CompilerParams(dimension_semantics=("parallel",), vmem_limit_bytes=VMEM_LIMIT),
        name="out_ffn",
    )(gla, dsa, x2, wo, n1, n2, w1, w2, n3)


def _rope_tables(seq):
    pos = jnp.arange(seq, dtype=F32)[:, None]

    def table(dim, reps):
        inv = ROPE_THETA ** (-jnp.arange(0, dim, 2, dtype=F32) / dim)
        ang = pos * inv[None, :]
        cos, sin = jnp.cos(ang), jnp.sin(ang)
        return (jnp.tile(jnp.concatenate([cos, cos], axis=-1), (1, reps)),
                jnp.tile(jnp.concatenate([-sin, sin], axis=-1), (1, reps)))

    cos128, sin128 = table(DSA_HEAD_DIM, 1)
    cos64, sin64 = table(IDX_DIM, LANES // IDX_DIM)
    return cos128, sin128, cos64, sin64


def _pack_w_in(w_in):
    sizes = (GLA_QK, GLA_QK, GLA_V, GLA_GATE_RANK, GLA_V, DSA_Q, DSA_HEAD_DIM, DSA_HEAD_DIM, IDX_Q, IDX_DIM, IDX_HEADS)
    offs = np.concatenate([[0], np.cumsum(sizes)])
    g_q, g_k, g_v, g_lr, g_r, d_q, d_k, d_v, i_q, i_k, i_w = [w_in[:, offs[i]:offs[i + 1]] for i in range(len(sizes))]
    pad = jnp.zeros((w_in.shape[0], LANES - IDX_DIM - IDX_HEADS - GLA_GATE_RANK), w_in.dtype)
    return jnp.concatenate([g_q, g_k, g_v, g_r, d_q, d_k, d_v, i_q, i_k, i_w, g_lr, pad], axis=1)


def kernel(x, norm_mix_pre, w_in, gla_gate_w2, gla_gate_b, gla_norm_w, idx_k_norm_w, idx_k_norm_b,
           w_out, norm_mix_post, norm_ffn_pre, w_ff1, w_ff2, norm_ffn_post):
    batch, seq, d = x.shape
    tabs = _rope_tables(seq)
    x2 = x.reshape(batch * seq, d)
    for l in range(w_in.shape[0]):
        w_pack = _pack_w_in(w_in[l]).astype(BF16)
        w2_pad = jnp.zeros((LANES, GLA_QK), F32).at[M_LR:M_LR + GLA_GATE_RANK].set(gla_gate_w2[l]).astype(BF16)
        lnw = jnp.zeros((1, LANES), F32).at[0, :IDX_DIM].set(idx_k_norm_w[l])
        lnb = jnp.zeros((1, LANES), F32).at[0, :IDX_DIM].set(idx_k_norm_b[l])
        (gq, gk, gv, gr, glog, dq, dk, dvt, iq, ik2, misc) = _projection(
            x2, norm_mix_pre[l][None], w_pack, w2_pad, gla_gate_b[l][None], lnw, lnb, tabs, seq)
        gla = _gla(gq, gk, glog, gv, gr, gla_norm_w[l][None], batch, seq)
        dsa = _dsa(iq, misc, dq, ik2, dk, dvt, batch, seq)
        x2 = _out_ffn(gla, dsa, x2, w_out[l].astype(BF16), norm_mix_post[l][None], norm_ffn_pre[l][None],
                      w_ff1[l].astype(BF16), w_ff2[l].astype(BF16), norm_ffn_post[l][None])
    return x2.reshape(batch, seq, d)
```

```python
import functools

import jax
import jax.numpy as jnp
import numpy as np
from jax import lax
from jax.experimental import pallas as pl
from jax.experimental.pallas import tpu as pltpu

F32 = jnp.float32
BF16 = jnp.bfloat16

EPS = 1e-6
ROPE_THETA = 10000.0
CHUNK = 64
GLA_HEADS = 4
GLA_DK = 64
GLA_DV = 128
GLA_GATE_RANK = 16
GLA_TAU = 16.0
DSA_HEADS = 4
DSA_HEAD_DIM = 128
IDX_HEADS = 4
IDX_DIM = 64
TOPK_MAX = 256

GLA_QK = GLA_HEADS * GLA_DK
GLA_V = GLA_HEADS * GLA_DV
DSA_Q = DSA_HEADS * DSA_HEAD_DIM
IDX_Q = IDX_HEADS * IDX_DIM

LANES = 128
VMEM_LIMIT = 56 * 1024 * 1024

C_GQ = 0
C_GK = C_GQ + GLA_QK
C_GV = C_GK + GLA_QK
C_GR = C_GV + GLA_V
C_DQ = C_GR + GLA_V
C_DK = C_DQ + DSA_Q
C_DV = C_DK + DSA_HEAD_DIM
C_IQ = C_DV + DSA_HEAD_DIM
C_MISC = C_IQ + IDX_Q
D_PACK = C_MISC + LANES
M_IW = IDX_DIM
M_LR = IDX_DIM + IDX_HEADS

PROJ_TM = 1024
GLA_TS = 1024
DSA_QB = 256
DSA_TK = 256
DSA_SEQS = 1
DSA_UNROLL = 8
DSA_UNROLL_SCORE = 8
DSA_UNROLL_COUNT = 8
DSA_BISECT_COARSE = 12
DSA_WALK_FIXED = 1
DSA_BISECT = 8
PACKED_ROWS = 16
BF16_STEP = 2.0 ** -7
TINY = 2.0 ** -120
DSA_ONES = PACKED_ROWS
DSA_VROWS = DSA_HEAD_DIM + DSA_ONES
DSA_Q_SCALE = DSA_HEAD_DIM ** -0.5 * 1.4426950408889634
FFN_TM = 1024
FFN_TF = 512


def _rms(x, w):
    return x * lax.rsqrt(jnp.mean(x * x, axis=-1, keepdims=True) + EPS) * w


def _proj_kernel(x_ref, nw_ref, w_ref, w2_ref, gb_ref, lnw_ref, lnb_ref,
                 cos128_ref, sin128_ref, cos64_ref, sin64_ref,
                 gq_ref, gk_ref, gv_ref, gr_ref, glog_ref,
                 dq_ref, dk_ref, dv_ref, iq_ref, ik_ref, misc_ref):
    hb = _rms(x_ref[...], nw_ref[...]).astype(BF16)

    def proj(lo, width):
        return jnp.dot(hb, w_ref[:, lo:lo + width], preferred_element_type=F32)

    cos128 = cos128_ref[...]
    sin128 = sin128_ref[...]
    cos64 = cos64_ref[...]
    sin64 = sin64_ref[...]
    lane = lax.broadcasted_iota(jnp.int32, (1, LANES), 1)
    first_half64 = (lane % IDX_DIM) < (IDX_DIM // 2)

    def rope128(t):
        return t * cos128 + pltpu.roll(t, LANES // 2, 1) * sin128

    def rope64(t):
        rot = jnp.where(first_half64, pltpu.roll(t, LANES - IDX_DIM // 2, 1), pltpu.roll(t, IDX_DIM // 2, 1))
        return t * cos64 + rot * sin64

    misc = proj(C_MISC, LANES)
    misc_ref[...] = misc * (IDX_HEADS ** -0.5 * IDX_DIM ** -0.5)

    is_ik = lane < IDX_DIM
    mu = jnp.sum(jnp.where(is_ik, misc, 0.0), axis=-1, keepdims=True) * (1.0 / IDX_DIM)
    d = jnp.where(is_ik, misc - mu, 0.0)
    var = jnp.sum(d * d, axis=-1, keepdims=True) * (1.0 / IDX_DIM)
    y = d * lax.rsqrt(var + EPS) * lnw_ref[...] + lnb_ref[...]
    yr = rope64(y)
    ik_ref[...] = (yr + pltpu.roll(yr, LANES // 2, 1)).astype(BF16)

    z = jnp.dot(misc.astype(BF16), w2_ref[...], preferred_element_type=F32) + gb_ref[...]
    glog_ref[...] = (jnp.minimum(z, 0.0) - jnp.log1p(jnp.exp(-jnp.abs(z)))) * (1.0 / GLA_TAU)

    dq = proj(C_DQ, DSA_Q)
    for h in range(DSA_HEADS):
        sl = slice(h * LANES, (h + 1) * LANES)
        dq_ref[:, sl] = (rope128(dq[:, sl]) * DSA_Q_SCALE).astype(BF16)
    kvq = proj(C_DK, 2 * DSA_HEAD_DIM + IDX_Q)
    dk_ref[...] = rope128(kvq[:, :DSA_HEAD_DIM]).astype(BF16)
    for c in range(IDX_Q // LANES):
        src = slice(C_IQ - C_DK + c * LANES, C_IQ - C_DK + (c + 1) * LANES)
        iq_ref[:, c * LANES:(c + 1) * LANES] = rope64(kvq[:, src]).astype(BF16)

    dv = kvq[:, C_DV - C_DK:C_IQ - C_DK]
    ones = jnp.ones((DSA_ONES, DSA_TK), BF16)
    for c in range(dv_ref.shape[0]):
        dv_t = dv[c * DSA_TK:(c + 1) * DSA_TK, :].T.astype(BF16)
        dv_ref[c] = jnp.concatenate([dv_t, ones], axis=0)

    gqk = proj(C_GQ, 2 * GLA_QK)
    gq_ref[...] = gqk[:, :GLA_QK]
    gk_ref[...] = gqk[:, GLA_QK:]
    gv_ref[...] = proj(C_GV, GLA_V).astype(BF16)
    gr_ref[...] = proj(C_GR, GLA_V)


def _projection(x2, nw, w_pack, w2_pad, gb, lnw, lnb, tabs, seq):
    t, d = x2.shape
    tm = PROJ_TM
    assert t % tm == 0 and seq % tm == 0
    tiles_per_seq = seq // tm

    def row(i):
        return (i, 0)

    def const(i):
        return (0, 0)

    def pos(i):
        return (i % tiles_per_seq, 0)

    widths = [(GLA_QK, F32), (GLA_QK, F32), (GLA_V, BF16), (GLA_V, F32), (GLA_QK, F32),
              (DSA_Q, BF16), (DSA_HEAD_DIM, BF16), (DSA_HEAD_DIM, BF16), (IDX_Q, BF16),
              (LANES, BF16), (LANES, F32)]
    DV_SLOT = 7
    assert tm % DSA_TK == 0
    dvt_spec = pl.BlockSpec((tm // DSA_TK, DSA_VROWS, DSA_TK), lambda i: (i, 0, 0))
    dvt_shape = jax.ShapeDtypeStruct((t // DSA_TK, DSA_VROWS, DSA_TK), BF16)
    return pl.pallas_call(
        _proj_kernel,
        grid=(t // tm,),
        in_specs=[
            pl.BlockSpec((tm, d), row),
            pl.BlockSpec((1, d), const),
            pl.BlockSpec((d, D_PACK), const),
            pl.BlockSpec((LANES, GLA_QK), const),
            pl.BlockSpec((1, GLA_QK), const),
            pl.BlockSpec((1, LANES), const),
            pl.BlockSpec((1, LANES), const),
            pl.BlockSpec((tm, LANES), pos),
            pl.BlockSpec((tm, LANES), pos),
            pl.BlockSpec((tm, LANES), pos),
            pl.BlockSpec((tm, LANES), pos),
        ],
        out_specs=[dvt_spec if n == DV_SLOT else pl.BlockSpec((tm, w), row) for n, (w, _) in enumerate(widths)],
        out_shape=[dvt_shape if n == DV_SLOT else jax.ShapeDtypeStruct((t, w), dt)
                   for n, (w, dt) in enumerate(widths)],
        compiler_params=pltpu.CompilerParams(dimension_semantics=("parallel",), vmem_limit_bytes=VMEM_LIMIT),
        name="proj",
    )(x2, nw, w_pack, w2_pad, gb, lnw, lnb, *tabs)


def _gla_kernel(q_ref, k_ref, g_ref, v_ref, r_ref, nw_ref, o_ref, st_ref):
    @pl.when(pl.program_id(1) == 0)
    def _():
        st_ref[...] = jnp.zeros_like(st_ref)

    ts = q_ref.shape[0]
    n_chunks = ts // CHUNK
    n_pairs = GLA_HEADS // 2
    ri = lax.broadcasted_iota(jnp.int32, (CHUNK, CHUNK), 0)
    ci = lax.broadcasted_iota(jnp.int32, (CHUNK, CHUNK), 1)
    causal = ri >= ci
    tri = jnp.where(causal, 1.0, 0.0).astype(BF16)
    lane = lax.broadcasted_iota(jnp.int32, (1, LANES), 1)
    half_mask = [lane < GLA_DK, lane >= GLA_DK]
    nw = nw_ref[...]
    nt = (((1,), (1,)), ((), ()))
    tn = (((0,), (0,)), ((), ()))
    chunks = [slice(c * CHUNK, (c + 1) * CHUNK) for c in range(n_chunks)]
    heads = [(p, e) for p in range(n_pairs) for e in range(2)]

    g = g_ref[...]
    g1 = g.astype(BF16)
    g2 = (g - g1.astype(F32)).astype(BF16)
    g3 = (g - g1.astype(F32) - g2.astype(F32)).astype(BF16)
    bcum = [jnp.dot(tri, g1[rows], preferred_element_type=F32)
            + jnp.dot(tri, g2[rows], preferred_element_type=F32)
            + jnp.dot(tri, g3[rows], preferred_element_type=F32) for rows in chunks]
    b_last = [b[CHUNK - 1:CHUNK, :] for b in bcum]
    decay = [jnp.exp(b) for b in b_last]
    q_dec = [q_ref[rows, :] * (GLA_DK ** -0.5) * jnp.exp(b) for rows, b in zip(chunks, bcum)]
    k_inv = [(k_ref[rows, :] * jnp.exp(-b)).astype(BF16) for rows, b in zip(chunks, bcum)]
    k_end = [k_ref[rows, :] * jnp.exp(bl - b) for rows, b, bl in zip(chunks, bcum, b_last)]

    def pair_lanes(x, p):
        return x[:, p * LANES:(p + 1) * LANES]

    qm = [[jnp.where(half_mask[e], pair_lanes(q_dec[c], p), 0.0).astype(BF16) for p, e in heads]
          for c in range(n_chunks)]
    attn = [[jnp.where(causal, lax.dot_general(qm[c][i], pair_lanes(k_inv[c], p), nt, preferred_element_type=F32),
                       0.0).astype(BF16) for i, (p, e) in enumerate(heads)] for c in range(n_chunks)]
    o_intra = [[jnp.dot(attn[c][i], v_ref[chunks[c], i * GLA_DV:(i + 1) * GLA_DV], preferred_element_type=F32)
                for i in range(GLA_HEADS)] for c in range(n_chunks)]
    upd = []
    for c in range(n_chunks):
        per_pair = []
        for p in range(n_pairs):
            ke = pair_lanes(k_end[c], p)
            ke2 = jnp.concatenate([jnp.where(half_mask[e], ke, 0.0).astype(BF16) for e in range(2)], axis=0)
            v2 = jnp.concatenate([v_ref[chunks[c], (2 * p + e) * GLA_DV:(2 * p + e + 1) * GLA_DV] for e in range(2)],
                                 axis=0)
            per_pair.append(lax.dot_general(v2, ke2, tn, preferred_element_type=F32))
        upd.append(per_pair)

    st_in = []
    for p in range(n_pairs):
        st = st_ref[p]
        per_chunk = []
        for c in range(n_chunks):
            per_chunk.append(st.astype(BF16))
            st = st * pair_lanes(decay[c], p) + upd[c][p]
        st_ref[p] = st
        st_in.append(per_chunk)

    for c in range(n_chunks):
        for i, (p, e) in enumerate(heads):
            o = o_intra[c][i] + lax.dot_general(qm[c][i], st_in[p][c], nt, preferred_element_type=F32)
            o = _rms(o, nw)
            r = r_ref[chunks[c], i * GLA_DV:(i + 1) * GLA_DV]
            o_ref[chunks[c], i * GLA_DV:(i + 1) * GLA_DV] = (o * (r * jax.nn.sigmoid(r))).astype(BF16)


def _gla(gq, gk, glog, gv, gr, nw, batch, seq):
    ts = GLA_TS
    assert seq % ts == 0 and ts % CHUNK == 0
    n = seq // ts

    def row(b, s):
        return (b * n + s, 0)

    return pl.pallas_call(
        _gla_kernel,
        grid=(batch, n),
        in_specs=[
            pl.BlockSpec((ts, GLA_QK), row),
            pl.BlockSpec((ts, GLA_QK), row),
            pl.BlockSpec((ts, GLA_QK), row),
            pl.BlockSpec((ts, GLA_V), row),
            pl.BlockSpec((ts, GLA_V), row),
            pl.BlockSpec((1, GLA_DV), lambda b, s: (0, 0)),
        ],
        out_specs=pl.BlockSpec((ts, GLA_V), row),
        out_shape=jax.ShapeDtypeStruct((batch * seq, GLA_V), BF16),
        scratch_shapes=[pltpu.VMEM((GLA_HEADS // 2, GLA_DV, 2 * GLA_DK), F32)],
        compiler_params=pltpu.CompilerParams(dimension_semantics=("parallel", "arbitrary"),
                                             vmem_limit_bytes=VMEM_LIMIT),
        name="gla",
    )(gq, gk, glog, gv, gr, nw)


def _dsa_kernel(qi_ref, misc_ref, q_ref, ik_ref, k_ref, vt_ref, o_ref,
                sc_ref, acc_ref, qm_ref, lg_ref, scb_ref, *, n_sel):
    nb, qb = qi_ref.shape[0], qi_ref.shape[1]
    width = nb * qb
    tk = sc_ref.shape[1]
    j = pl.program_id(1)
    n_tiles = ((j + 1) * qb + tk - 1) // tk
    nt = (((1,), (1,)), ((), ()))
    seq_lanes = [slice(b * qb, (b + 1) * qb) for b in range(nb)]

    col = lax.broadcasted_iota(jnp.int32, (1, width), 1) % qb
    limit = j * qb + (col // CHUNK + 1) * CHUNK
    k_eff = jnp.minimum(limit, n_sel).astype(F32)
    key_in_tile = lax.broadcasted_iota(jnp.int32, (tk, qb), 0)
    limit_q = limit[:, :qb]
    lane = lax.broadcasted_iota(jnp.int32, (1, LANES), 1)

    wh = []
    for b in range(nb):
        misc_t = misc_ref[b].T
        for h in range(IDX_HEADS):
            pair = qi_ref[b, :, (h // 2) * LANES:(h // 2 + 1) * LANES].astype(F32)
            in_head = (lane // IDX_DIM) == (h % 2)
            qm_ref[b, h] = jnp.where(in_head, pair, 0.0).astype(BF16)
        wh.append([misc_t[M_IW + h:M_IW + h + 1, :] for h in range(IDX_HEADS)])

    def for_each_tile_group(run, init, unroll):
        carry = lax.fori_loop(0, n_tiles // unroll, lambda i, c: run(unroll * i, unroll, c), init)
        first = (n_tiles // unroll) * unroll
        tail = unroll // 2
        while tail >= 1:
            carry = lax.cond((n_tiles & tail) != 0, functools.partial(run, first, tail), lambda c: c, carry)
            first = first + (n_tiles & tail)
            tail //= 2
        return carry

    def for_each_tile(body, init, unroll=DSA_UNROLL_COUNT):
        def run(first, count, carry):
            for k in range(count):
                carry = body(first + k, carry)
            return carry
        return for_each_tile_group(run, init, unroll)

    def fold(op, a):
        return op(a.reshape(a.shape[0] // 8, 8, a.shape[-1]), axis=0)

    def fold_max(a):
        return fold(jnp.max, a)

    def fold_sum(a):
        return fold(jnp.sum, a)

    def key_rows(ref, b, t):
        return ref[b, pl.ds(pl.multiple_of(t * tk, tk), tk), :]

    def score_tile(t, carry):
        pmin, pmax = carry
        mins, maxs = [], []
        for b in range(nb):
            kt = key_rows(ik_ref, b, t)
            sc = None
            for h in range(IDX_HEADS):
                s = lax.dot_general(kt, qm_ref[b, h], nt, preferred_element_type=F32)
                term = wh[b][h] * jnp.maximum(s, 0.0)
                sc = term if sc is None else sc + term
            scm = jnp.where(key_in_tile < limit_q - t * tk, sc, -jnp.inf)
            sc_ref[t, :, seq_lanes[b]] = scm
            scb_ref[t, :, seq_lanes[b]] = scm.astype(BF16)
            mins.append(fold(jnp.min, sc))
            maxs.append(fold_max(scm))
        return (jnp.minimum(pmin, jnp.concatenate(mins, axis=1)),
                jnp.maximum(pmax, jnp.concatenate(maxs, axis=1)))

    pmin, pmax = for_each_tile(
        score_tile, (jnp.full((8, width), jnp.inf, F32), jnp.full((8, width), -jnp.inf, F32)), DSA_UNROLL_SCORE)
    vmin = jnp.min(pmin, axis=0, keepdims=True)
    vmax = jnp.max(pmax, axis=0, keepdims=True)

    def count_ge(thr):
        def body(t, acc):
            return acc + fold_sum(jnp.where(sc_ref[t] >= thr, 1.0, 0.0))
        acc = for_each_tile(body, jnp.zeros((8, width), F32))
        return jnp.sum(acc, axis=0, keepdims=True)

    def max_below(bound):
        def body(t, acc):
            s = sc_ref[t]
            return jnp.maximum(acc, fold_max(jnp.where(s < bound, s, -jnp.inf)))
        acc = for_each_tile(body, jnp.full((8, width), -jnp.inf, F32))
        return jnp.max(acc, axis=0, keepdims=True)

    def count_ge_coarse(thr_b):
        one, zero = jnp.ones((), BF16), jnp.zeros((), BF16)

        def body(t, acc):
            ind = jnp.where(scb_ref[t] >= thr_b, one, zero)
            parts = [ind[r * PACKED_ROWS:(r + 1) * PACKED_ROWS, :] for r in range(tk // PACKED_ROWS)]
            while len(parts) > 1:
                parts = [a + b for a, b in zip(parts[::2], parts[1::2])]
            return acc + parts[0]
        acc = for_each_tile(body, jnp.zeros((PACKED_ROWS, width), BF16))
        return jnp.sum(acc.astype(F32), axis=0, keepdims=True)

    def bisect_coarse(_, carry):
        lo, hb, hi = carry
        mid_b = (0.5 * lo + 0.5 * hb).astype(BF16)
        mid = mid_b.astype(F32)
        up = count_ge_coarse(mid_b) >= k_eff
        return jnp.where(up, mid, lo), jnp.where(up, hb, mid), jnp.where(up, hi, mid)

    lo, _, hi = lax.fori_loop(0, DSA_BISECT_COARSE, bisect_coarse,
                              (vmin, vmax, jnp.full((1, width), jnp.inf, F32)))
    lo = lo - jnp.abs(lo) * BF16_STEP - TINY
    hi = hi + jnp.abs(hi) * BF16_STEP + TINY
    hb = jnp.minimum(hi, vmax)
    c_hi = count_ge(hi)

    def bisect(_, carry):
        lo, hb, hi, c_hi = carry
        mid = 0.5 * lo + 0.5 * hb
        c = count_ge(mid)
        up = c >= k_eff
        return (jnp.where(up, mid, lo), jnp.where(up, hb, mid),
                jnp.where(up, hi, mid), jnp.where(up, c_hi, c))

    lo, hb, hi, c_hi = lax.fori_loop(0, DSA_BISECT, bisect, (lo, hb, hi, c_hi))

    max_steps = n_tiles * tk

    def walk_step(state):
        hi, c_hi, thr, done = state
        cand = max_below(hi)
        c = count_ge(cand)
        ok = jnp.logical_and(done < 0.5, c >= k_eff)
        thr = jnp.where(ok, cand, thr)
        done = jnp.where(ok, 1.0, done)
        moving = done < 0.5
        return jnp.where(moving, cand, hi), jnp.where(moving, c, c_hi), thr, done

    def remaining(state):
        return jnp.sum(jnp.where(state[3] < 0.5, 1.0, 0.0)).astype(jnp.int32)

    state = (hi, c_hi, vmin, jnp.zeros((1, width), F32))
    for _ in range(DSA_WALK_FIXED):
        state = walk_step(state)

    def not_done(carry):
        return jnp.logical_and(carry[0] > 0, carry[1] < max_steps)

    def step(carry):
        state = walk_step(carry[2])
        return remaining(state), carry[1] + 1, state

    _, _, (_, c_hi, thr, _) = lax.while_loop(not_done, step, (remaining(state), jnp.int32(0), state))
    n_tie = k_eff - c_hi

    ti = lax.broadcasted_iota(jnp.int32, (tk, tk), 0)
    tj = lax.broadcasted_iota(jnp.int32, (tk, tk), 1)
    tri_incl = jnp.where(tj <= ti, 1.0, 0.0).astype(BF16)
    neg = -1e30

    def attend(b):
        thr_b, n_tie_b = thr[:, seq_lanes[b]], n_tie[:, seq_lanes[b]]

        def logit_tile(t, carry):
            seen, pmax = carry
            s = sc_ref[t, :, seq_lanes[b]]
            tie = s == thr_b
            incl = jnp.dot(tri_incl, jnp.where(tie, 1.0, 0.0).astype(BF16), preferred_element_type=F32)
            sel = jnp.logical_or(s > thr_b, jnp.logical_and(tie, incl <= n_tie_b - seen))
            bias = jnp.where(sel, 0.0, neg)
            kt = key_rows(k_ref, b, t)
            pmax_out = []
            for h in range(DSA_HEADS):
                lg = lax.dot_general(kt, q_ref[b, :, h * LANES:(h + 1) * LANES], nt,
                                     preferred_element_type=F32) + bias
                lg_ref[t, h] = lg
                pmax_out.append(jnp.maximum(pmax[h], fold_max(lg)))
            return seen + incl[tk - 1:tk, :], tuple(pmax_out)

        pmax0 = tuple(jnp.full((8, qb), neg, F32) for _ in range(DSA_HEADS))
        _, pmax = for_each_tile(logit_tile, (jnp.zeros((1, qb), F32), pmax0), DSA_UNROLL)
        m = [jnp.max(pmax[h], axis=0, keepdims=True) for h in range(DSA_HEADS)]

        acc_ref[...] = jnp.zeros_like(acc_ref)

        def value_tiles(tiles):
            vt = jnp.concatenate([vt_ref[b, t] for t in tiles], axis=1)
            for h in range(DSA_HEADS):
                lanes = slice(h * qb, (h + 1) * qb)
                p = jnp.concatenate([jnp.exp2((lg_ref[t, h] - m[h]).astype(BF16)) for t in tiles], axis=0)
                acc_ref[:, lanes] += jnp.dot(vt, p, preferred_element_type=F32)

        @pl.loop(0, n_tiles // DSA_UNROLL)
        def _(i):
            value_tiles([DSA_UNROLL * i + k for k in range(DSA_UNROLL)])

        first = (n_tiles // DSA_UNROLL) * DSA_UNROLL
        tail = DSA_UNROLL // 2
        while tail >= 1:
            pl.when((n_tiles & tail) != 0)(functools.partial(value_tiles, [first + k for k in range(tail)]))
            first = first + (n_tiles & tail)
            tail //= 2

        for h in range(DSA_HEADS):
            lanes = slice(h * qb, (h + 1) * qb)
            out_t = acc_ref[:DSA_HEAD_DIM, lanes] / acc_ref[DSA_HEAD_DIM:DSA_HEAD_DIM + 1, lanes]
            o_ref[b, :, h * LANES:(h + 1) * LANES] = out_t.T.astype(BF16)

    for b in range(nb):
        attend(b)


def _dsa(iq, misc, dq, ik2, dk, dvt, batch, seq):
    qb, tk = DSA_QB, DSA_TK
    assert seq % qb == 0 and seq % tk == 0 and qb % CHUNK == 0
    nq = seq // qb
    nkt = seq // tk
    n_sel = min(TOPK_MAX, seq // 4)

    nb = DSA_SEQS if batch % DSA_SEQS == 0 else 1

    def per_seq(a):
        return a.reshape((batch, seq) + a.shape[1:])

    def qrow(b, j):
        return (b, j, 0)

    def kv(b, j):
        return (b, 0, 0)

    out = pl.pallas_call(
        functools.partial(_dsa_kernel, n_sel=n_sel),
        grid=(batch // nb, nq),
        in_specs=[
            pl.BlockSpec((nb, qb, IDX_Q), qrow),
            pl.BlockSpec((nb, qb, LANES), qrow),
            pl.BlockSpec((nb, qb, DSA_Q), qrow),
            pl.BlockSpec((nb, seq, LANES), kv),
            pl.BlockSpec((nb, seq, DSA_HEAD_DIM), kv),
            pl.BlockSpec((nb, nkt, DSA_VROWS, tk), lambda b, j: (b, 0, 0, 0)),
        ],
        out_specs=pl.BlockSpec((nb, qb, DSA_Q), qrow),
        out_shape=jax.ShapeDtypeStruct((batch, seq, DSA_Q), BF16),
        scratch_shapes=[pltpu.VMEM((nkt, tk, nb * qb), F32),
                        pltpu.VMEM((DSA_VROWS, DSA_HEADS * qb), F32),
                        pltpu.VMEM((nb, IDX_HEADS, qb, LANES), BF16),
                        pltpu.VMEM((nkt, DSA_HEADS, tk, qb), F32),
                        pltpu.VMEM((nkt, tk, nb * qb), BF16)],
        compiler_params=pltpu.CompilerParams(dimension_semantics=("parallel", "arbitrary"),
                                             vmem_limit_bytes=VMEM_LIMIT),
        name="dsa",
    )(per_seq(iq), per_seq(misc), per_seq(dq), per_seq(ik2), per_seq(dk),
      dvt.reshape((batch, nkt) + dvt.shape[1:]))
    return out.reshape(batch * seq, DSA_Q)


def _out_ffn_kernel(gla_ref, dsa_ref, x_ref, wo_ref, n1_ref, n2_ref, w1_ref, w2_ref, n3_ref, o_ref):
    mixed = (jnp.dot(gla_ref[...], wo_ref[:GLA_V, :], preferred_element_type=F32)
             + jnp.dot(dsa_ref[...], wo_ref[GLA_V:, :], preferred_element_type=F32))
    x1 = x_ref[...] + _rms(mixed, n1_ref[...])
    h = _rms(x1, n2_ref[...]).astype(BF16)
    d_ff = w1_ref.shape[1]
    acc = jnp.zeros(x1.shape, F32)
    for f in range(d_ff // FFN_TF):
        cols = slice(f * FFN_TF, (f + 1) * FFN_TF)
        a = jnp.maximum(jnp.dot(h, w1_ref[:, cols], preferred_element_type=F32), 0.0)
        acc = acc + jnp.dot((a * a).astype(BF16), w2_ref[cols, :], preferred_element_type=F32)
    o_ref[...] = x1 + _rms(acc, n3_ref[...])


def _out_ffn(gla, dsa, x2, wo, n1, n2, w1, w2, n3):
    t, d = x2.shape
    tm = FFN_TM
    d_ff = w1.shape[1]
    assert t % tm == 0 and d_ff % FFN_TF == 0

    def row(i):
        return (i, 0)

    def const(i):
        return (0, 0)

    def resident(shape):
        return pl.BlockSpec(shape, const, pipeline_mode=pl.Buffered(1))

    return pl.pallas_call(
        _out_ffn_kernel,
        grid=(t // tm,),
        in_specs=[
            pl.BlockSpec((tm, GLA_V), row),
            pl.BlockSpec((tm, DSA_Q), row),
            pl.BlockSpec((tm, d), row),
            resident((GLA_V + DSA_Q, d)),
            resident((1, d)),
            resident((1, d)),
            resident((d, d_ff)),
            resident((d_ff, d)),
            resident((1, d)),
        ],
        out_specs=pl.BlockSpec((tm, d), row),
        out_shape=jax.ShapeDtypeStruct((t, d), F32),
        compiler_params=pltpu.CompilerParams(dimension_semantics=("parallel",), vmem_limit_bytes=VMEM_LIMIT),
        name="out_ffn",
    )(gla, dsa, x2, wo, n1, n2, w1, w2, n3)


def _rope_tables(seq):
    pos = jnp.arange(seq, dtype=F32)[:, None]

    def table(dim, reps):
        inv = ROPE_THETA ** (-jnp.arange(0, dim, 2, dtype=F32) / dim)
        ang = pos * inv[None, :]
        cos, sin = jnp.cos(ang), jnp.sin(ang)
        return (jnp.tile(jnp.concatenate([cos, cos], axis=-1), (1, reps)),
                jnp.tile(jnp.concatenate([-sin, sin], axis=-1), (1, reps)))

    cos128, sin128 = table(DSA_HEAD_DIM, 1)
    cos64, sin64 = table(IDX_DIM, LANES // IDX_DIM)
    return cos128, sin128, cos64, sin64


def _pack_w_in(w_in):
    sizes = (GLA_QK, GLA_QK, GLA_V, GLA_GATE_RANK, GLA_V, DSA_Q, DSA_HEAD_DIM, DSA_HEAD_DIM, IDX_Q, IDX_DIM, IDX_HEADS)
    offs = np.concatenate([[0], np.cumsum(sizes)])
    g_q, g_k, g_v, g_lr, g_r, d_q, d_k, d_v, i_q, i_k, i_w = [w_in[:, offs[i]:offs[i + 1]] for i in range(len(sizes))]
    pad = jnp.zeros((w_in.shape[0], LANES - IDX_DIM - IDX_HEADS - GLA_GATE_RANK), w_in.dtype)
    return jnp.concatenate([g_q, g_k, g_v, g_r, d_q, d_k, d_v, i_q, i_k, i_w, g_lr, pad], axis=1)


def kernel(x, norm_mix_pre, w_in, gla_gate_w2, gla_gate_b, gla_norm_w, idx_k_norm_w, idx_k_norm_b,
           w_out, norm_mix_post, norm_ffn_pre, w_ff1, w_ff2, norm_ffn_post):
    batch, seq, d = x.shape
    tabs = _rope_tables(seq)
    x2 = x.reshape(batch * seq, d)
    for l in range(w_in.shape[0]):
        w_pack = _pack_w_in(w_in[l]).astype(BF16)
        w2_pad = jnp.zeros((LANES, GLA_QK), F32).at[M_LR:M_LR + GLA_GATE_RANK].set(gla_gate_w2[l]).astype(BF16)
        lnw = jnp.zeros((1, LANES), F32).at[0, :IDX_DIM].set(idx_k_norm_w[l])
        lnb = jnp.zeros((1, LANES), F32).at[0, :IDX_DIM].set(idx_k_norm_b[l])
        (gq, gk, gv, gr, glog, dq, dk, dvt, iq, ik2, misc) = _projection(
            x2, norm_mix_pre[l][None], w_pack, w2_pad, gla_gate_b[l][None], lnw, lnb, tabs, seq)
        gla = _gla(gq, gk, glog, gv, gr, gla_norm_w[l][None], batch, seq)
        dsa = _dsa(iq, misc, dq, ik2, dk, dvt, batch, seq)
        x2 = _out_ffn(gla, dsa, x2, w_out[l].astype(BF16), norm_mix_post[l][None], norm_ffn_pre[l][None],
                      w_ff1[l].astype(BF16), w_ff2[l].astype(BF16), norm_ffn_post[l][None])
    return x2.reshape(batch, seq, d)
```

```python
import functools

import jax
import jax.numpy as jnp
import numpy as np
from jax import lax
from jax.experimental import pallas as pl
from jax.experimental.pallas import tpu as pltpu

F32 = jnp.float32
BF16 = jnp.bfloat16

EPS = 1e-6
ROPE_THETA = 10000.0
CHUNK = 64
GLA_HEADS = 4
GLA_DK = 64
GLA_DV = 128
GLA_GATE_RANK = 16
GLA_TAU = 16.0
DSA_HEADS = 4
DSA_HEAD_DIM = 128
IDX_HEADS = 4
IDX_DIM = 64
TOPK_MAX = 256

GLA_QK = GLA_HEADS * GLA_DK
GLA_V = GLA_HEADS * GLA_DV
DSA_Q = DSA_HEADS * DSA_HEAD_DIM
IDX_Q = IDX_HEADS * IDX_DIM

LANES = 128
VMEM_LIMIT = 56 * 1024 * 1024

C_GQ = 0
C_GK = C_GQ + GLA_QK
C_GV = C_GK + GLA_QK
C_GR = C_GV + GLA_V
C_DQ = C_GR + GLA_V
C_DK = C_DQ + DSA_Q
C_DV = C_DK + DSA_HEAD_DIM
C_IQ = C_DV + DSA_HEAD_DIM
C_MISC = C_IQ + IDX_Q
D_PACK = C_MISC + LANES
M_IW = IDX_DIM
M_LR = IDX_DIM + IDX_HEADS

PROJ_TM = 1024
GLA_TS = 1024
DSA_QB = 256
DSA_TK = 256
DSA_SEQS = 1
DSA_UNROLL = 8
DSA_UNROLL_SCORE = 8
DSA_UNROLL_COUNT = 8
DSA_BISECT_COARSE = 12
DSA_WALK_FIXED = 2
DSA_BISECT = 8
PACKED_ROWS = 16
BF16_STEP = 2.0 ** -7
TINY = 2.0 ** -120
DSA_ONES = PACKED_ROWS
DSA_VROWS = DSA_HEAD_DIM + DSA_ONES
DSA_Q_SCALE = DSA_HEAD_DIM ** -0.5 * 1.4426950408889634
FFN_TM = 1024
FFN_TF = 512


def _rms(x, w):
    return x * lax.rsqrt(jnp.mean(x * x, axis=-1, keepdims=True) + EPS) * w


def _proj_kernel(x_ref, nw_ref, w_ref, w2_ref, gb_ref, lnw_ref, lnb_ref,
                 cos128_ref, sin128_ref, cos64_ref, sin64_ref,
                 gq_ref, gk_ref, gv_ref, gr_ref, glog_ref,
                 dq_ref, dk_ref, dv_ref, iq_ref, ik_ref, misc_ref):
    hb = _rms(x_ref[...], nw_ref[...]).astype(BF16)

    def proj(lo, width):
        return jnp.dot(hb, w_ref[:, lo:lo + width], preferred_element_type=F32)

    cos128 = cos128_ref[...]
    sin128 = sin128_ref[...]
    cos64 = cos64_ref[...]
    sin64 = sin64_ref[...]
    lane = lax.broadcasted_iota(jnp.int32, (1, LANES), 1)
    first_half64 = (lane % IDX_DIM) < (IDX_DIM // 2)

    def rope128(t):
        return t * cos128 + pltpu.roll(t, LANES // 2, 1) * sin128

    def rope64(t):
        rot = jnp.where(first_half64, pltpu.roll(t, LANES - IDX_DIM // 2, 1), pltpu.roll(t, IDX_DIM // 2, 1))
        return t * cos64 + rot * sin64

    misc = proj(C_MISC, LANES)
    misc_ref[...] = misc * (IDX_HEADS ** -0.5 * IDX_DIM ** -0.5)

    is_ik = lane < IDX_DIM
    mu = jnp.sum(jnp.where(is_ik, misc, 0.0), axis=-1, keepdims=True) * (1.0 / IDX_DIM)
    d = jnp.where(is_ik, misc - mu, 0.0)
    var = jnp.sum(d * d, axis=-1, keepdims=True) * (1.0 / IDX_DIM)
    y = d * lax.rsqrt(var + EPS) * lnw_ref[...] + lnb_ref[...]
    yr = rope64(y)
    ik_ref[...] = (yr + pltpu.roll(yr, LANES // 2, 1)).astype(BF16)

    z = jnp.dot(misc.astype(BF16), w2_ref[...], preferred_element_type=F32) + gb_ref[...]
    glog_ref[...] = (jnp.minimum(z, 0.0) - jnp.log1p(jnp.exp(-jnp.abs(z)))) * (1.0 / GLA_TAU)

    dq = proj(C_DQ, DSA_Q)
    for h in range(DSA_HEADS):
        sl = slice(h * LANES, (h + 1) * LANES)
        dq_ref[:, sl] = (rope128(dq[:, sl]) * DSA_Q_SCALE).astype(BF16)
    kvq = proj(C_DK, 2 * DSA_HEAD_DIM + IDX_Q)
    dk_ref[...] = rope128(kvq[:, :DSA_HEAD_DIM]).astype(BF16)
    for c in range(IDX_Q // LANES):
        src = slice(C_IQ - C_DK + c * LANES, C_IQ - C_DK + (c + 1) * LANES)
        iq_ref[:, c * LANES:(c + 1) * LANES] = rope64(kvq[:, src]).astype(BF16)

    dv = kvq[:, C_DV - C_DK:C_IQ - C_DK]
    ones = jnp.ones((DSA_ONES, DSA_TK), BF16)
    for c in range(dv_ref.shape[0]):
        dv_t = dv[c * DSA_TK:(c + 1) * DSA_TK, :].T.astype(BF16)
        dv_ref[c] = jnp.concatenate([dv_t, ones], axis=0)

    gqk = proj(C_GQ, 2 * GLA_QK)
    gq_ref[...] = gqk[:, :GLA_QK]
    gk_ref[...] = gqk[:, GLA_QK:]
    gv_ref[...] = proj(C_GV, GLA_V).astype(BF16)
    gr_ref[...] = proj(C_GR, GLA_V)


def _projection(x2, nw, w_pack, w2_pad, gb, lnw, lnb, tabs, seq):
    t, d = x2.shape
    tm = PROJ_TM
    assert t % tm == 0 and seq % tm == 0
    tiles_per_seq = seq // tm

    def row(i):
        return (i, 0)

    def const(i):
        return (0, 0)

    def pos(i):
        return (i % tiles_per_seq, 0)

    widths = [(GLA_QK, F32), (GLA_QK, F32), (GLA_V, BF16), (GLA_V, F32), (GLA_QK, F32),
              (DSA_Q, BF16), (DSA_HEAD_DIM, BF16), (DSA_HEAD_DIM, BF16), (IDX_Q, BF16),
              (LANES, BF16), (LANES, F32)]
    DV_SLOT = 7
    assert tm % DSA_TK == 0
    dvt_spec = pl.BlockSpec((tm // DSA_TK, DSA_VROWS, DSA_TK), lambda i: (i, 0, 0))
    dvt_shape = jax.ShapeDtypeStruct((t // DSA_TK, DSA_VROWS, DSA_TK), BF16)
    return pl.pallas_call(
        _proj_kernel,
        grid=(t // tm,),
        in_specs=[
            pl.BlockSpec((tm, d), row),
            pl.BlockSpec((1, d), const),
            pl.BlockSpec((d, D_PACK), const),
            pl.BlockSpec((LANES, GLA_QK), const),
            pl.BlockSpec((1, GLA_QK), const),
            pl.BlockSpec((1, LANES), const),
            pl.BlockSpec((1, LANES), const),
            pl.BlockSpec((tm, LANES), pos),
            pl.BlockSpec((tm, LANES), pos),
            pl.BlockSpec((tm, LANES), pos),
            pl.BlockSpec((tm, LANES), pos),
        ],
        out_specs=[dvt_spec if n == DV_SLOT else pl.BlockSpec((tm, w), row) for n, (w, _) in enumerate(widths)],
        out_shape=[dvt_shape if n == DV_SLOT else jax.ShapeDtypeStruct((t, w), dt)
                   for n, (w, dt) in enumerate(widths)],
        compiler_params=pltpu.CompilerParams(dimension_semantics=("parallel",), vmem_limit_bytes=VMEM_LIMIT),
        name="proj",
    )(x2, nw, w_pack, w2_pad, gb, lnw, lnb, *tabs)


def _gla_kernel(q_ref, k_ref, g_ref, v_ref, r_ref, nw_ref, o_ref, st_ref):
    @pl.when(pl.program_id(1) == 0)
    def _():
        st_ref[...] = jnp.zeros_like(st_ref)

    ts = q_ref.shape[0]
    n_chunks = ts // CHUNK
    n_pairs = GLA_HEADS // 2
    ri = lax.broadcasted_iota(jnp.int32, (CHUNK, CHUNK), 0)
    ci = lax.broadcasted_iota(jnp.int32, (CHUNK, CHUNK), 1)
    causal = ri >= ci
    tri = jnp.where(causal, 1.0, 0.0).astype(BF16)
    lane = lax.broadcasted_iota(jnp.int32, (1, LANES), 1)
    half_mask = [lane < GLA_DK, lane >= GLA_DK]
    nw = nw_ref[...]
    nt = (((1,), (1,)), ((), ()))
    tn = (((0,), (0,)), ((), ()))
    chunks = [slice(c * CHUNK, (c + 1) * CHUNK) for c in range(n_chunks)]
    heads = [(p, e) for p in range(n_pairs) for e in range(2)]

    g = g_ref[...]
    g1 = g.astype(BF16)
    g2 = (g - g1.astype(F32)).astype(BF16)
    g3 = (g - g1.astype(F32) - g2.astype(F32)).astype(BF16)
    bcum = [jnp.dot(tri, g1[rows], preferred_element_type=F32)
            + jnp.dot(tri, g2[rows], preferred_element_type=F32)
            + jnp.dot(tri, g3[rows], preferred_element_type=F32) for rows in chunks]
    b_last = [b[CHUNK - 1:CHUNK, :] for b in bcum]
    decay = [jnp.exp(b) for b in b_last]
    q_dec = [q_ref[rows, :] * (GLA_DK ** -0.5) * jnp.exp(b) for rows, b in zip(chunks, bcum)]
    k_inv = [(k_ref[rows, :] * jnp.exp(-b)).astype(BF16) for rows, b in zip(chunks, bcum)]
    k_end = [k_ref[rows, :] * jnp.exp(bl - b) for rows, b, bl in zip(chunks, bcum, b_last)]

    def pair_lanes(x, p):
        return x[:, p * LANES:(p + 1) * LANES]

    qm = [[jnp.where(half_mask[e], pair_lanes(q_dec[c], p), 0.0).astype(BF16) for p, e in heads]
          for c in range(n_chunks)]
    attn = [[jnp.where(causal, lax.dot_general(qm[c][i], pair_lanes(k_inv[c], p), nt, preferred_element_type=F32),
                       0.0).astype(BF16) for i, (p, e) in enumerate(heads)] for c in range(n_chunks)]
    o_intra = [[jnp.dot(attn[c][i], v_ref[chunks[c], i * GLA_DV:(i + 1) * GLA_DV], preferred_element_type=F32)
                for i in range(GLA_HEADS)] for c in range(n_chunks)]
    upd = []
    for c in range(n_chunks):
        per_pair = []
        for p in range(n_pairs):
            ke = pair_lanes(k_end[c], p)
            ke2 = jnp.concatenate([jnp.where(half_mask[e], ke, 0.0).astype(BF16) for e in range(2)], axis=0)
            v2 = jnp.concatenate([v_ref[chunks[c], (2 * p + e) * GLA_DV:(2 * p + e + 1) * GLA_DV] for e in range(2)],
                                 axis=0)
            per_pair.append(lax.dot_general(v2, ke2, tn, preferred_element_type=F32))
        upd.append(per_pair)

    st_in = []
    for p in range(n_pairs):
        st = st_ref[p]
        per_chunk = []
        for c in range(n_chunks):
            per_chunk.append(st.astype(BF16))
            st = st * pair_lanes(decay[c], p) + upd[c][p]
        st_ref[p] = st
        st_in.append(per_chunk)

    for c in range(n_chunks):
        for i, (p, e) in enumerate(heads):
            o = o_intra[c][i] + lax.dot_general(qm[c][i], st_in[p][c], nt, preferred_element_type=F32)
            o = _rms(o, nw)
            r = r_ref[chunks[c], i * GLA_DV:(i + 1) * GLA_DV]
            o_ref[chunks[c], i * GLA_DV:(i + 1) * GLA_DV] = (o * (r * jax.nn.sigmoid(r))).astype(BF16)


def _gla(gq, gk, glog, gv, gr, nw, batch, seq):
    ts = GLA_TS
    assert seq % ts == 0 and ts % CHUNK == 0
    n = seq // ts

    def row(b, s):
        return (b * n + s, 0)

    return pl.pallas_call(
        _gla_kernel,
        grid=(batch, n),
        in_specs=[
            pl.BlockSpec((ts, GLA_QK), row),
            pl.BlockSpec((ts, GLA_QK), row),
            pl.BlockSpec((ts, GLA_QK), row),
            pl.BlockSpec((ts, GLA_V), row),
            pl.BlockSpec((ts, GLA_V), row),
            pl.BlockSpec((1, GLA_DV), lambda b, s: (0, 0)),
        ],
        out_specs=pl.BlockSpec((ts, GLA_V), row),
        out_shape=jax.ShapeDtypeStruct((batch * seq, GLA_V), BF16),
        scratch_shapes=[pltpu.VMEM((GLA_HEADS // 2, GLA_DV, 2 * GLA_DK), F32)],
        compiler_params=pltpu.CompilerParams(dimension_semantics=("parallel", "arbitrary"),
                                             vmem_limit_bytes=VMEM_LIMIT),
        name="gla",
    )(gq, gk, glog, gv, gr, nw)


def _dsa_kernel(qi_ref, misc_ref, q_ref, ik_ref, k_ref, vt_ref, o_ref,
                sc_ref, acc_ref, qm_ref, lg_ref, scb_ref, *, n_sel):
    nb, qb = qi_ref.shape[0], qi_ref.shape[1]
    width = nb * qb
    tk = sc_ref.shape[1]
    j = pl.program_id(1)
    n_tiles = ((j + 1) * qb + tk - 1) // tk
    nt = (((1,), (1,)), ((), ()))
    seq_lanes = [slice(b * qb, (b + 1) * qb) for b in range(nb)]

    col = lax.broadcasted_iota(jnp.int32, (1, width), 1) % qb
    limit = j * qb + (col // CHUNK + 1) * CHUNK
    k_eff = jnp.minimum(limit, n_sel).astype(F32)
    key_in_tile = lax.broadcasted_iota(jnp.int32, (tk, qb), 0)
    limit_q = limit[:, :qb]
    lane = lax.broadcasted_iota(jnp.int32, (1, LANES), 1)

    wh = []
    for b in range(nb):
        misc_t = misc_ref[b].T
        for h in range(IDX_HEADS):
            pair = qi_ref[b, :, (h // 2) * LANES:(h // 2 + 1) * LANES].astype(F32)
            in_head = (lane // IDX_DIM) == (h % 2)
            qm_ref[b, h] = jnp.where(in_head, pair, 0.0).astype(BF16)
        wh.append([misc_t[M_IW + h:M_IW + h + 1, :] for h in range(IDX_HEADS)])

    def for_each_tile_group(run, init, unroll):
        carry = lax.fori_loop(0, n_tiles // unroll, lambda i, c: run(unroll * i, unroll, c), init)
        first = (n_tiles // unroll) * unroll
        tail = unroll // 2
        while tail >= 1:
            carry = lax.cond((n_tiles & tail) != 0, functools.partial(run, first, tail), lambda c: c, carry)
            first = first + (n_tiles & tail)
            tail //= 2
        return carry

    def for_each_tile(body, init, unroll=DSA_UNROLL_COUNT):
        def run(first, count, carry):
            for k in range(count):
                carry = body(first + k, carry)
            return carry
        return for_each_tile_group(run, init, unroll)

    def fold(op, a):
        return op(a.reshape(a.shape[0] // 8, 8, a.shape[-1]), axis=0)

    def fold_max(a):
        return fold(jnp.max, a)

    def fold_sum(a):
        return fold(jnp.sum, a)

    def key_rows(ref, b, t):
        return ref[b, pl.ds(pl.multiple_of(t * tk, tk), tk), :]

    def score_tile(t, carry):
        pmin, pmax = carry
        mins, maxs = [], []
        for b in range(nb):
            kt = key_rows(ik_ref, b, t)
            sc = None
            for h in range(IDX_HEADS):
                s = lax.dot_general(kt, qm_ref[b, h], nt, preferred_element_type=F32)
                term = wh[b][h] * jnp.maximum(s, 0.0)
                sc = term if sc is None else sc + term
            scm = jnp.where(key_in_tile < limit_q - t * tk, sc, -jnp.inf)
            sc_ref[t, :, seq_lanes[b]] = scm
            scb_ref[t, :, seq_lanes[b]] = scm.astype(BF16)
            mins.append(fold(jnp.min, sc))
            maxs.append(fold_max(scm))
        return (jnp.minimum(pmin, jnp.concatenate(mins, axis=1)),
                jnp.maximum(pmax, jnp.concatenate(maxs, axis=1)))

    pmin, pmax = for_each_tile(
        score_tile, (jnp.full((8, width), jnp.inf, F32), jnp.full((8, width), -jnp.inf, F32)), DSA_UNROLL_SCORE)
    vmin = jnp.min(pmin, axis=0, keepdims=True)
    vmax = jnp.max(pmax, axis=0, keepdims=True)

    def count_ge(thr):
        def body(t, acc):
            return acc + fold_sum(jnp.where(sc_ref[t] >= thr, 1.0, 0.0))
        acc = for_each_tile(body, jnp.zeros((8, width), F32))
        return jnp.sum(acc, axis=0, keepdims=True)

    def max_below(bound):
        def body(t, acc):
            s = sc_ref[t]
            return jnp.maximum(acc, fold_max(jnp.where(s < bound, s, -jnp.inf)))
        acc = for_each_tile(body, jnp.full((8, width), -jnp.inf, F32))
        return jnp.max(acc, axis=0, keepdims=True)

    def count_ge_coarse(thr_b):
        one, zero = jnp.ones((), BF16), jnp.zeros((), BF16)

        def body(t, acc):
            ind = jnp.where(scb_ref[t] >= thr_b, one, zero)
            parts = [ind[r * PACKED_ROWS:(r + 1) * PACKED_ROWS, :] for r in range(tk // PACKED_ROWS)]
            while len(parts) > 1:
                parts = [a + b for a, b in zip(parts[::2], parts[1::2])]
            return acc + parts[0]
        acc = for_each_tile(body, jnp.zeros((PACKED_ROWS, width), BF16))
        return jnp.sum(acc.astype(F32), axis=0, keepdims=True)

    def bisect_coarse(_, carry):
        lo, hb, hi = carry
        mid_b = (0.5 * lo + 0.5 * hb).astype(BF16)
        mid = mid_b.astype(F32)
        up = count_ge_coarse(mid_b) >= k_eff
        return jnp.where(up, mid, lo), jnp.where(up, hb, mid), jnp.where(up, hi, mid)

    lo, _, hi = lax.fori_loop(0, DSA_BISECT_COARSE, bisect_coarse,
                              (vmin, vmax, jnp.full((1, width), jnp.inf, F32)))
    lo = lo - jnp.abs(lo) * BF16_STEP - TINY
    hi = hi + jnp.abs(hi) * BF16_STEP + TINY
    hb = jnp.minimum(hi, vmax)
    c_hi = count_ge(hi)

    def bisect(_, carry):
        lo, hb, hi, c_hi = carry
        mid = 0.5 * lo + 0.5 * hb
        c = count_ge(mid)
        up = c >= k_eff
        return (jnp.where(up, mid, lo), jnp.where(up, hb, mid),
                jnp.where(up, hi, mid), jnp.where(up, c_hi, c))

    lo, hb, hi, c_hi = lax.fori_loop(0, DSA_BISECT, bisect, (lo, hb, hi, c_hi))

    max_steps = n_tiles * tk

    def walk_step(state):
        hi, c_hi, thr, done = state
        cand = max_below(hi)
        c = count_ge(cand)
        ok = jnp.logical_and(done < 0.5, c >= k_eff)
        thr = jnp.where(ok, cand, thr)
        done = jnp.where(ok, 1.0, done)
        moving = done < 0.5
        return jnp.where(moving, cand, hi), jnp.where(moving, c, c_hi), thr, done

    def remaining(state):
        return jnp.sum(jnp.where(state[3] < 0.5, 1.0, 0.0)).astype(jnp.int32)

    state = (hi, c_hi, vmin, jnp.zeros((1, width), F32))
    for _ in range(DSA_WALK_FIXED):
        state = walk_step(state)

    def not_done(carry):
        return jnp.logical_and(carry[0] > 0, carry[1] < max_steps)

    def step(carry):
        state = walk_step(carry[2])
        return remaining(state), carry[1] + 1, state

    _, _, (_, c_hi, thr, _) = lax.while_loop(not_done, step, (remaining(state), jnp.int32(0), state))
    n_tie = k_eff - c_hi

    ti = lax.broadcasted_iota(jnp.int32, (tk, tk), 0)
    tj = lax.broadcasted_iota(jnp.int32, (tk, tk), 1)
    tri_incl = jnp.where(tj <= ti, 1.0, 0.0).astype(BF16)
    neg = -1e30

    def attend(b):
        thr_b, n_tie_b = thr[:, seq_lanes[b]], n_tie[:, seq_lanes[b]]

        def logit_tile(t, carry):
            seen, pmax = carry
            s = sc_ref[t, :, seq_lanes[b]]
            tie = s == thr_b
            incl = jnp.dot(tri_incl, jnp.where(tie, 1.0, 0.0).astype(BF16), preferred_element_type=F32)
            sel = jnp.logical_or(s > thr_b, jnp.logical_and(tie, incl <= n_tie_b - seen))
            bias = jnp.where(sel, 0.0, neg)
            kt = key_rows(k_ref, b, t)
            pmax_out = []
            for h in range(DSA_HEADS):
                lg = lax.dot_general(kt, q_ref[b, :, h * LANES:(h + 1) * LANES], nt,
                                     preferred_element_type=F32) + bias
                lg_ref[t, h] = lg
                pmax_out.append(jnp.maximum(pmax[h], fold_max(lg)))
            return seen + incl[tk - 1:tk, :], tuple(pmax_out)

        pmax0 = tuple(jnp.full((8, qb), neg, F32) for _ in range(DSA_HEADS))
        _, pmax = for_each_tile(logit_tile, (jnp.zeros((1, qb), F32), pmax0), DSA_UNROLL)
        m = [jnp.max(pmax[h], axis=0, keepdims=True) for h in range(DSA_HEADS)]

        acc_ref[...] = jnp.zeros_like(acc_ref)

        def value_tiles(tiles):
            vt = jnp.concatenate([vt_ref[b, t] for t in tiles], axis=1)
            for h in range(DSA_HEADS):
                lanes = slice(h * qb, (h + 1) * qb)
                p = jnp.concatenate([jnp.exp2((lg_ref[t, h] - m[h]).astype(BF16)) for t in tiles], axis=0)
                acc_ref[:, lanes] += jnp.dot(vt, p, preferred_element_type=F32)

        @pl.loop(0, n_tiles // DSA_UNROLL)
        def _(i):
            value_tiles([DSA_UNROLL * i + k for k in range(DSA_UNROLL)])

        first = (n_tiles // DSA_UNROLL) * DSA_UNROLL
        tail = DSA_UNROLL // 2
        while tail >= 1:
            pl.when((n_tiles & tail) != 0)(functools.partial(value_tiles, [first + k for k in range(tail)]))
            first = first + (n_tiles & tail)
            tail //= 2

        for h in range(DSA_HEADS):
            lanes = slice(h * qb, (h + 1) * qb)
            out_t = acc_ref[:DSA_HEAD_DIM, lanes] / acc_ref[DSA_HEAD_DIM:DSA_HEAD_DIM + 1, lanes]
            o_ref[b, :, h * LANES:(h + 1) * LANES] = out_t.T.astype(BF16)

    for b in range(nb):
        attend(b)


def _dsa(iq, misc, dq, ik2, dk, dvt, batch, seq):
    qb, tk = DSA_QB, DSA_TK
    assert seq % qb == 0 and seq % tk == 0 and qb % CHUNK == 0
    nq = seq // qb
    nkt = seq // tk
    n_sel = min(TOPK_MAX, seq // 4)

    nb = DSA_SEQS if batch % DSA_SEQS == 0 else 1

    def per_seq(a):
        return a.reshape((batch, seq) + a.shape[1:])

    def qrow(b, j):
        return (b, j, 0)

    def kv(b, j):
        return (b, 0, 0)

    out = pl.pallas_call(
        functools.partial(_dsa_kernel, n_sel=n_sel),
        grid=(batch // nb, nq),
        in_specs=[
            pl.BlockSpec((nb, qb, IDX_Q), qrow),
            pl.BlockSpec((nb, qb, LANES), qrow),
            pl.BlockSpec((nb, qb, DSA_Q), qrow),
            pl.BlockSpec((nb, seq, LANES), kv),
            pl.BlockSpec((nb, seq, DSA_HEAD_DIM), kv),
            pl.BlockSpec((nb, nkt, DSA_VROWS, tk), lambda b, j: (b, 0, 0, 0)),
        ],
        out_specs=pl.BlockSpec((nb, qb, DSA_Q), qrow),
        out_shape=jax.ShapeDtypeStruct((batch, seq, DSA_Q), BF16),
        scratch_shapes=[pltpu.VMEM((nkt, tk, nb * qb), F32),
                        pltpu.VMEM((DSA_VROWS, DSA_HEADS * qb), F32),
                        pltpu.VMEM((nb, IDX_HEADS, qb, LANES), BF16),
                        pltpu.VMEM((nkt, DSA_HEADS, tk, qb), F32),
                        pltpu.VMEM((nkt, tk, nb * qb), BF16)],
        compiler_params=pltpu.CompilerParams(dimension_semantics=("parallel", "arbitrary"),
                                             vmem_limit_bytes=VMEM_LIMIT),
        name="dsa",
    )(per_seq(iq), per_seq(misc), per_seq(dq), per_seq(ik2), per_seq(dk),
      dvt.reshape((batch, nkt) + dvt.shape[1:]))
    return out.reshape(batch * seq, DSA_Q)


def _out_ffn_kernel(gla_ref, dsa_ref, x_ref, wo_ref, n1_ref, n2_ref, w1_ref, w2_ref, n3_ref, o_ref):
    mixed = (jnp.dot(gla_ref[...], wo_ref[:GLA_V, :], preferred_element_type=F32)
             + jnp.dot(dsa_ref[...], wo_ref[GLA_V:, :], preferred_element_type=F32))
    x1 = x_ref[...] + _rms(mixed, n1_ref[...])
    h = _rms(x1, n2_ref[...]).astype(BF16)
    d_ff = w1_ref.shape[1]
    acc = jnp.zeros(x1.shape, F32)
    for f in range(d_ff // FFN_TF):
        cols = slice(f * FFN_TF, (f + 1) * FFN_TF)
        a = jnp.maximum(jnp.dot(h, w1_ref[:, cols], preferred_element_type=F32), 0.0)
        acc = acc + jnp.dot((a * a).astype(BF16), w2_ref[cols, :], preferred_element_type=F32)
    o_ref[...] = x1 + _rms(acc, n3_ref[...])


def _out_ffn(gla, dsa, x2, wo, n1, n2, w1, w2, n3):
    t, d = x2.shape
    tm = FFN_TM
    d_ff = w1.shape[1]
    assert t % tm == 0 and d_ff % FFN_TF == 0

    def row(i):
        return (i, 0)

    def const(i):
        return (0, 0)

    def resident(shape):
        return pl.BlockSpec(shape, const, pipeline_mode=pl.Buffered(1))

    return pl.pallas_call(
        _out_ffn_kernel,
        grid=(t // tm,),
        in_specs=[
            pl.BlockSpec((tm, GLA_V), row),
            pl.BlockSpec((tm, DSA_Q), row),
            pl.BlockSpec((tm, d), row),
            resident((GLA_V + DSA_Q, d)),
            resident((1, d)),
            resident((1, d)),
            resident((d, d_ff)),
            resident((d_ff, d)),
            resident((1, d)),
        ],
        out_specs=pl.BlockSpec((tm, d), row),
        out_shape=jax.ShapeDtypeStruct((t, d), F32),
        compiler_params=pltpu.CompilerParams(dimension_semantics=("parallel",), vmem_limit_bytes=VMEM_LIMIT),
        name="out_ffn",
    )(gla, dsa, x2, wo, n1, n2, w1, w2, n3)


def _rope_tables(seq):
    pos = jnp.arange(seq, dtype=F32)[:, None]

    def table(dim, reps):
        inv = ROPE_THETA ** (-jnp.arange(0, dim, 2, dtype=F32) / dim)
        ang = pos * inv[None, :]
        cos, sin = jnp.cos(ang), jnp.sin(ang)
        return (jnp.tile(jnp.concatenate([cos, cos], axis=-1), (1, reps)),
                jnp.tile(jnp.concatenate([-sin, sin], axis=-1), (1, reps)))

    cos128, sin128 = table(DSA_HEAD_DIM, 1)
    cos64, sin64 = table(IDX_DIM, LANES // IDX_DIM)
    return cos128, sin128, cos64, sin64


def _pack_w_in(w_in):
    sizes = (GLA_QK, GLA_QK, GLA_V, GLA_GATE_RANK, GLA_V, DSA_Q, DSA_HEAD_DIM, DSA_HEAD_DIM, IDX_Q, IDX_DIM, IDX_HEADS)
    offs = np.concatenate([[0], np.cumsum(sizes)])
    g_q, g_k, g_v, g_lr, g_r, d_q, d_k, d_v, i_q, i_k, i_w = [w_in[:, offs[i]:offs[i + 1]] for i in range(len(sizes))]
    pad = jnp.zeros((w_in.shape[0], LANES - IDX_DIM - IDX_HEADS - GLA_GATE_RANK), w_in.dtype)
    return jnp.concatenate([g_q, g_k, g_v, g_r, d_q, d_k, d_v, i_q, i_k, i_w, g_lr, pad], axis=1)


def kernel(x, norm_mix_pre, w_in, gla_gate_w2, gla_gate_b, gla_norm_w, idx_k_norm_w, idx_k_norm_b,
           w_out, norm_mix_post, norm_ffn_pre, w_ff1, w_ff2, norm_ffn_post):
    batch, seq, d = x.shape
    tabs = _rope_tables(seq)
    x2 = x.reshape(batch * seq, d)
    for l in range(w_in.shape[0]):
        w_pack = _pack_w_in(w_in[l]).astype(BF16)
        w2_pad = jnp.zeros((LANES, GLA_QK), F32).at[M_LR:M_LR + GLA_GATE_RANK].set(gla_gate_w2[l]).astype(BF16)
        lnw = jnp.zeros((1, LANES), F32).at[0, :IDX_DIM].set(idx_k_norm_w[l])
        lnb = jnp.zeros((1, LANES), F32).at[0, :IDX_DIM].set(idx_k_norm_b[l])
        (gq, gk, gv, gr, glog, dq, dk, dvt, iq, ik2, misc) = _projection(
            x2, norm_mix_pre[l][None], w_pack, w2_pad, gla_gate_b[l][None], lnw, lnb, tabs, seq)
        gla = _gla(gq, gk, glog, gv, gr, gla_norm_w[l][None], batch, seq)
        dsa = _dsa(iq, misc, dq, ik2, dk, dvt, batch, seq)
        x2 = _out_ffn(gla, dsa, x2, w_out[l].astype(BF16), norm_mix_post[l][None], norm_ffn_pre[l][None],
                      w_ff1[l].astype(BF16), w_ff2[l].astype(BF16), norm_ffn_post[l][None])
    return x2.reshape(batch, seq, d)
```

```python
import functools

import jax
import jax.numpy as jnp
import numpy as np
from jax import lax
from jax.experimental import pallas as pl
from jax.experimental.pallas import tpu as pltpu

F32 = jnp.float32
BF16 = jnp.bfloat16

EPS = 1e-6
ROPE_THETA = 10000.0
CHUNK = 64
GLA_HEADS = 4
GLA_DK = 64
GLA_DV = 128
GLA_GATE_RANK = 16
GLA_TAU = 16.0
DSA_HEADS = 4
DSA_HEAD_DIM = 128
IDX_HEADS = 4
IDX_DIM = 64
TOPK_MAX = 256

GLA_QK = GLA_HEADS * GLA_DK
GLA_V = GLA_HEADS * GLA_DV
DSA_Q = DSA_HEADS * DSA_HEAD_DIM
IDX_Q = IDX_HEADS * IDX_DIM

LANES = 128
VMEM_LIMIT = 56 * 1024 * 1024

C_GQ = 0
C_GK = C_GQ + GLA_QK
C_GV = C_GK + GLA_QK
C_GR = C_GV + GLA_V
C_DQ = C_GR + GLA_V
C_DK = C_DQ + DSA_Q
C_DV = C_DK + DSA_HEAD_DIM
C_IQ = C_DV + DSA_HEAD_DIM
C_MISC = C_IQ + IDX_Q
D_PACK = C_MISC + LANES
M_IW = IDX_DIM
M_LR = IDX_DIM + IDX_HEADS

PROJ_TM = 1024
DSA_QB = 256
DSA_TK = 256
DSA_SEQS = 1
DSA_UNROLL = 8
DSA_UNROLL_SCORE = 8
DSA_UNROLL_COUNT = 8
DSA_BISECT_COARSE = 12
DSA_WALK_FIXED = 2
DSA_BISECT = 8
PACKED_ROWS = 16
BF16_STEP = 2.0 ** -7
TINY = 2.0 ** -120
DSA_ONES = PACKED_ROWS
DSA_VROWS = DSA_HEAD_DIM + DSA_ONES
DSA_Q_SCALE = DSA_HEAD_DIM ** -0.5 * 1.4426950408889634
FFN_TM = 1024
FFN_TF = 512


def _rms(x, w):
    return x * lax.rsqrt(jnp.mean(x * x, axis=-1, keepdims=True) + EPS) * w


def _proj_kernel(x_ref, nw_ref, w_ref, w2_ref, gb_ref, lnw_ref, lnb_ref,
                 cos128_ref, sin128_ref, cos64_ref, sin64_ref,
                 gnw_ref,
                 gla_ref, dq_ref, dk_ref, dv_ref, iq_ref, ik_ref, misc_ref,
                 gq_ref, gk_ref, gv_ref, gr_ref, glog_ref, st_ref, *, tiles_per_seq):
    hb = _rms(x_ref[...], nw_ref[...]).astype(BF16)

    def proj(lo, width):
        return jnp.dot(hb, w_ref[:, lo:lo + width], preferred_element_type=F32)

    cos128 = cos128_ref[...]
    sin128 = sin128_ref[...]
    cos64 = cos64_ref[...]
    sin64 = sin64_ref[...]
    lane = lax.broadcasted_iota(jnp.int32, (1, LANES), 1)
    first_half64 = (lane % IDX_DIM) < (IDX_DIM // 2)

    def rope128(t):
        return t * cos128 + pltpu.roll(t, LANES // 2, 1) * sin128

    def rope64(t):
        rot = jnp.where(first_half64, pltpu.roll(t, LANES - IDX_DIM // 2, 1), pltpu.roll(t, IDX_DIM // 2, 1))
        return t * cos64 + rot * sin64

    misc = proj(C_MISC, LANES)
    misc_ref[...] = misc * (IDX_HEADS ** -0.5 * IDX_DIM ** -0.5)

    is_ik = lane < IDX_DIM
    mu = jnp.sum(jnp.where(is_ik, misc, 0.0), axis=-1, keepdims=True) * (1.0 / IDX_DIM)
    d = jnp.where(is_ik, misc - mu, 0.0)
    var = jnp.sum(d * d, axis=-1, keepdims=True) * (1.0 / IDX_DIM)
    y = d * lax.rsqrt(var + EPS) * lnw_ref[...] + lnb_ref[...]
    yr = rope64(y)
    ik_ref[...] = (yr + pltpu.roll(yr, LANES // 2, 1)).astype(BF16)

    z = jnp.dot(misc.astype(BF16), w2_ref[...], preferred_element_type=F32) + gb_ref[...]
    glog_ref[...] = (jnp.minimum(z, 0.0) - jnp.log1p(jnp.exp(-jnp.abs(z)))) * (1.0 / GLA_TAU)

    dq = proj(C_DQ, DSA_Q)
    for h in range(DSA_HEADS):
        sl = slice(h * LANES, (h + 1) * LANES)
        dq_ref[:, sl] = (rope128(dq[:, sl]) * DSA_Q_SCALE).astype(BF16)
    kvq = proj(C_DK, 2 * DSA_HEAD_DIM + IDX_Q)
    dk_ref[...] = rope128(kvq[:, :DSA_HEAD_DIM]).astype(BF16)
    for c in range(IDX_Q // LANES):
        src = slice(C_IQ - C_DK + c * LANES, C_IQ - C_DK + (c + 1) * LANES)
        iq_ref[:, c * LANES:(c + 1) * LANES] = rope64(kvq[:, src]).astype(BF16)

    dv = kvq[:, C_DV - C_DK:C_IQ - C_DK]
    ones = jnp.ones((DSA_ONES, DSA_TK), BF16)
    for c in range(dv_ref.shape[0]):
        dv_t = dv[c * DSA_TK:(c + 1) * DSA_TK, :].T.astype(BF16)
        dv_ref[c] = jnp.concatenate([dv_t, ones], axis=0)

    gqk = proj(C_GQ, 2 * GLA_QK)
    gq_ref[...] = gqk[:, :GLA_QK]
    gk_ref[...] = gqk[:, GLA_QK:]
    gv_ref[...] = proj(C_GV, GLA_V).astype(BF16)
    gr_ref[...] = proj(C_GR, GLA_V)
    _gla_body(gq_ref, gk_ref, glog_ref, gv_ref, gr_ref, gnw_ref, gla_ref, st_ref,
              pl.program_id(0) % tiles_per_seq == 0)


def _projection(x2, nw, w_pack, w2_pad, gb, lnw, lnb, tabs, gla_nw, seq):
    t, d = x2.shape
    tm = PROJ_TM
    assert t % tm == 0 and seq % tm == 0
    tiles_per_seq = seq // tm

    def row(i):
        return (i, 0)

    def const(i):
        return (0, 0)

    def pos(i):
        return (i % tiles_per_seq, 0)

    widths = [(GLA_V, BF16), (DSA_Q, BF16), (DSA_HEAD_DIM, BF16), (DSA_HEAD_DIM, BF16), (IDX_Q, BF16),
              (LANES, BF16), (LANES, F32)]
    DV_SLOT = 3
    assert tm % DSA_TK == 0 and tm % CHUNK == 0
    dvt_spec = pl.BlockSpec((tm // DSA_TK, DSA_VROWS, DSA_TK), lambda i: (i, 0, 0))
    dvt_shape = jax.ShapeDtypeStruct((t // DSA_TK, DSA_VROWS, DSA_TK), BF16)
    gla_scratch = [pltpu.VMEM((tm, GLA_QK), F32), pltpu.VMEM((tm, GLA_QK), F32), pltpu.VMEM((tm, GLA_V), BF16),
                   pltpu.VMEM((tm, GLA_V), F32), pltpu.VMEM((tm, GLA_QK), F32),
                   pltpu.VMEM((GLA_HEADS // 2, GLA_DV, 2 * GLA_DK), F32)]
    return pl.pallas_call(
        functools.partial(_proj_kernel, tiles_per_seq=tiles_per_seq),
        grid=(t // tm,),
        in_specs=[
            pl.BlockSpec((tm, d), row),
            pl.BlockSpec((1, d), const),
            pl.BlockSpec((d, D_PACK), const),
            pl.BlockSpec((LANES, GLA_QK), const),
            pl.BlockSpec((1, GLA_QK), const),
            pl.BlockSpec((1, LANES), const),
            pl.BlockSpec((1, LANES), const),
            pl.BlockSpec((tm, LANES), pos),
            pl.BlockSpec((tm, LANES), pos),
            pl.BlockSpec((tm, LANES), pos),
            pl.BlockSpec((tm, LANES), pos),
            pl.BlockSpec((1, GLA_DV), const),
        ],
        out_specs=[dvt_spec if n == DV_SLOT else pl.BlockSpec((tm, w), row) for n, (w, _) in enumerate(widths)],
        out_shape=[dvt_shape if n == DV_SLOT else jax.ShapeDtypeStruct((t, w), dt)
                   for n, (w, dt) in enumerate(widths)],
        scratch_shapes=gla_scratch,
        compiler_params=pltpu.CompilerParams(dimension_semantics=("arbitrary",), vmem_limit_bytes=VMEM_LIMIT),
        name="proj",
    )(x2, nw, w_pack, w2_pad, gb, lnw, lnb, *tabs, gla_nw)


def _gla_body(q_ref, k_ref, g_ref, v_ref, r_ref, nw_ref, o_ref, st_ref, first):
    @pl.when(first)
    def _():
        st_ref[...] = jnp.zeros_like(st_ref)

    ts = q_ref.shape[0]
    n_chunks = ts // CHUNK
    n_pairs = GLA_HEADS // 2
    ri = lax.broadcasted_iota(jnp.int32, (CHUNK, CHUNK), 0)
    ci = lax.broadcasted_iota(jnp.int32, (CHUNK, CHUNK), 1)
    causal = ri >= ci
    tri = jnp.where(causal, 1.0, 0.0).astype(BF16)
    lane = lax.broadcasted_iota(jnp.int32, (1, LANES), 1)
    half_mask = [lane < GLA_DK, lane >= GLA_DK]
    nw = nw_ref[...]
    nt = (((1,), (1,)), ((), ()))
    tn = (((0,), (0,)), ((), ()))
    chunks = [slice(c * CHUNK, (c + 1) * CHUNK) for c in range(n_chunks)]
    heads = [(p, e) for p in range(n_pairs) for e in range(2)]

    g = g_ref[...]
    g1 = g.astype(BF16)
    g2 = (g - g1.astype(F32)).astype(BF16)
    g3 = (g - g1.astype(F32) - g2.astype(F32)).astype(BF16)
    bcum = [jnp.dot(tri, g1[rows], preferred_element_type=F32)
            + jnp.dot(tri, g2[rows], preferred_element_type=F32)
            + jnp.dot(tri, g3[rows], preferred_element_type=F32) for rows in chunks]
    b_last = [b[CHUNK - 1:CHUNK, :] for b in bcum]
    decay = [jnp.exp(b) for b in b_last]
    q_dec = [q_ref[rows, :] * (GLA_DK ** -0.5) * jnp.exp(b) for rows, b in zip(chunks, bcum)]
    k_inv = [(k_ref[rows, :] * jnp.exp(-b)).astype(BF16) for rows, b in zip(chunks, bcum)]
    k_end = [k_ref[rows, :] * jnp.exp(bl - b) for rows, b, bl in zip(chunks, bcum, b_last)]

    def pair_lanes(x, p):
        return x[:, p * LANES:(p + 1) * LANES]

    qm = [[jnp.where(half_mask[e], pair_lanes(q_dec[c], p), 0.0).astype(BF16) for p, e in heads]
          for c in range(n_chunks)]
    attn = [[jnp.where(causal, lax.dot_general(qm[c][i], pair_lanes(k_inv[c], p), nt, preferred_element_type=F32),
                       0.0).astype(BF16) for i, (p, e) in enumerate(heads)] for c in range(n_chunks)]
    o_intra = [[jnp.dot(attn[c][i], v_ref[chunks[c], i * GLA_DV:(i + 1) * GLA_DV], preferred_element_type=F32)
                for i in range(GLA_HEADS)] for c in range(n_chunks)]
    upd = []
    for c in range(n_chunks):
        per_pair = []
        for p in range(n_pairs):
            ke = pair_lanes(k_end[c], p)
            ke2 = jnp.concatenate([jnp.where(half_mask[e], ke, 0.0).astype(BF16) for e in range(2)], axis=0)
            v2 = jnp.concatenate([v_ref[chunks[c], (2 * p + e) * GLA_DV:(2 * p + e + 1) * GLA_DV] for e in range(2)],
                                 axis=0)
            per_pair.append(lax.dot_general(v2, ke2, tn, preferred_element_type=F32))
        upd.append(per_pair)

    st_in = []
    for p in range(n_pairs):
        st = st_ref[p]
        per_chunk = []
        for c in range(n_chunks):
            per_chunk.append(st.astype(BF16))
            st = st * pair_lanes(decay[c], p) + upd[c][p]
        st_ref[p] = st
        st_in.append(per_chunk)

    for c in range(n_chunks):
        for i, (p, e) in enumerate(heads):
            o = o_intra[c][i] + lax.dot_general(qm[c][i], st_in[p][c], nt, preferred_element_type=F32)
            o = _rms(o, nw)
            r = r_ref[chunks[c], i * GLA_DV:(i + 1) * GLA_DV]
            o_ref[chunks[c], i * GLA_DV:(i + 1) * GLA_DV] = (o * (r * jax.nn.sigmoid(r))).astype(BF16)


def _dsa_kernel(qi_ref, misc_ref, q_ref, ik_ref, k_ref, vt_ref, o_ref,
                sc_ref, acc_ref, qm_ref, lg_ref, scb_ref, *, n_sel):
    nb, qb = qi_ref.shape[0], qi_ref.shape[1]
    width = nb * qb
    tk = sc_ref.shape[1]
    j = pl.program_id(1)
    n_tiles = ((j + 1) * qb + tk - 1) // tk
    nt = (((1,), (1,)), ((), ()))
    seq_lanes = [slice(b * qb, (b + 1) * qb) for b in range(nb)]

    col = lax.broadcasted_iota(jnp.int32, (1, width), 1) % qb
    limit = j * qb + (col // CHUNK + 1) * CHUNK
    k_eff = jnp.minimum(limit, n_sel).astype(F32)
    key_in_tile = lax.broadcasted_iota(jnp.int32, (tk, qb), 0)
    limit_q = limit[:, :qb]
    lane = lax.broadcasted_iota(jnp.int32, (1, LANES), 1)

    wh = []
    for b in range(nb):
        misc_t = misc_ref[b].T
        for h in range(IDX_HEADS):
            pair = qi_ref[b, :, (h // 2) * LANES:(h // 2 + 1) * LANES].astype(F32)
            in_head = (lane // IDX_DIM) == (h % 2)
            qm_ref[b, h] = jnp.where(in_head, pair, 0.0).astype(BF16)
        wh.append([misc_t[M_IW + h:M_IW + h + 1, :] for h in range(IDX_HEADS)])

    def for_each_tile_group(run, init, unroll):
        carry = lax.fori_loop(0, n_tiles // unroll, lambda i, c: run(unroll * i, unroll, c), init)
        first = (n_tiles // unroll) * unroll
        tail = unroll // 2
        while tail >= 1:
            carry = lax.cond((n_tiles & tail) != 0, functools.partial(run, first, tail), lambda c: c, carry)
            first = first + (n_tiles & tail)
            tail //= 2
        return carry

    def for_each_tile(body, init, unroll=DSA_UNROLL_COUNT):
        def run(first, count, carry):
            for k in range(count):
                carry = body(first + k, carry)
            return carry
        return for_each_tile_group(run, init, unroll)

    def fold(op, a):
        return op(a.reshape(a.shape[0] // 8, 8, a.shape[-1]), axis=0)

    def fold_max(a):
        return fold(jnp.max, a)

    def fold_sum(a):
        return fold(jnp.sum, a)

    def key_rows(ref, b, t):
        return ref[b, pl.ds(pl.multiple_of(t * tk, tk), tk), :]

    def score_tile(t, carry):
        pmin, pmax = carry
        mins, maxs = [], []
        for b in range(nb):
            kt = key_rows(ik_ref, b, t)
            sc = None
            for h in range(IDX_HEADS):
                s = lax.dot_general(kt, qm_ref[b, h], nt, preferred_element_type=F32)
                term = wh[b][h] * jnp.maximum(s, 0.0)
                sc = term if sc is None else sc + term
            scm = jnp.where(key_in_tile < limit_q - t * tk, sc, -jnp.inf)
            sc_ref[t, :, seq_lanes[b]] = scm
            scb_ref[t, :, seq_lanes[b]] = scm.astype(BF16)
            mins.append(fold(jnp.min, sc))
            maxs.append(fold_max(scm))
        return (jnp.minimum(pmin, jnp.concatenate(mins, axis=1)),
                jnp.maximum(pmax, jnp.concatenate(maxs, axis=1)))

    pmin, pmax = for_each_tile(
        score_tile, (jnp.full((8, width), jnp.inf, F32), jnp.full((8, width), -jnp.inf, F32)), DSA_UNROLL_SCORE)
    vmin = jnp.min(pmin, axis=0, keepdims=True)
    vmax = jnp.max(pmax, axis=0, keepdims=True)

    def count_ge(thr):
        def body(t, acc):
            return acc + fold_sum(jnp.where(sc_ref[t] >= thr, 1.0, 0.0))
        acc = for_each_tile(body, jnp.zeros((8, width), F32))
        return jnp.sum(acc, axis=0, keepdims=True)

    def max_below(bound):
        def body(t, acc):
            s = sc_ref[t]
            return jnp.maximum(acc, fold_max(jnp.where(s < bound, s, -jnp.inf)))
        acc = for_each_tile(body, jnp.full((8, width), -jnp.inf, F32))
        return jnp.max(acc, axis=0, keepdims=True)

    def count_ge_coarse(thr_b):
        one, zero = jnp.ones((), BF16), jnp.zeros((), BF16)

        def body(t, acc):
            ind = jnp.where(scb_ref[t] >= thr_b, one, zero)
            parts = [ind[r * PACKED_ROWS:(r + 1) * PACKED_ROWS, :] for r in range(tk // PACKED_ROWS)]
            while len(parts) > 1:
                parts = [a + b for a, b in zip(parts[::2], parts[1::2])]
            return acc + parts[0]
        acc = for_each_tile(body, jnp.zeros((PACKED_ROWS, width), BF16))
        return jnp.sum(acc.astype(F32), axis=0, keepdims=True)

    def bisect_coarse(_, carry):
        lo, hb, hi = carry
        mid_b = (0.5 * lo + 0.5 * hb).astype(BF16)
        mid = mid_b.astype(F32)
        up = count_ge_coarse(mid_b) >= k_eff
        return jnp.where(up, mid, lo), jnp.where(up, hb, mid), jnp.where(up, hi, mid)

    lo, _, hi = lax.fori_loop(0, DSA_BISECT_COARSE, bisect_coarse,
                              (vmin, vmax, jnp.full((1, width), jnp.inf, F32)))
    lo = lo - jnp.abs(lo) * BF16_STEP - TINY
    hi = hi + jnp.abs(hi) * BF16_STEP + TINY
    hb = jnp.minimum(hi, vmax)
    c_hi = count_ge(hi)

    def bisect(_, carry):
        lo, hb, hi, c_hi = carry
        mid = 0.5 * lo + 0.5 * hb
        c = count_ge(mid)
        up = c >= k_eff
        return (jnp.where(up, mid, lo), jnp.where(up, hb, mid),
                jnp.where(up, hi, mid), jnp.where(up, c_hi, c))

    lo, hb, hi, c_hi = lax.fori_loop(0, DSA_BISECT, bisect, (lo, hb, hi, c_hi))

    max_steps = n_tiles * tk

    def walk_step(state):
        hi, c_hi, thr, done = state
        cand = max_below(hi)
        c = count_ge(cand)
        ok = jnp.logical_and(done < 0.5, c >= k_eff)
        thr = jnp.where(ok, cand, thr)
        done = jnp.where(ok, 1.0, done)
        moving = done < 0.5
        return jnp.where(moving, cand, hi), jnp.where(moving, c, c_hi), thr, done

    def remaining(state):
        return jnp.sum(jnp.where(state[3] < 0.5, 1.0, 0.0)).astype(jnp.int32)

    state = (hi, c_hi, vmin, jnp.zeros((1, width), F32))
    for _ in range(DSA_WALK_FIXED):
        state = walk_step(state)

    def not_done(carry):
        return jnp.logical_and(carry[0] > 0, carry[1] < max_steps)

    def step(carry):
        state = walk_step(carry[2])
        return remaining(state), carry[1] + 1, state

    _, _, (_, c_hi, thr, _) = lax.while_loop(not_done, step, (remaining(state), jnp.int32(0), state))
    n_tie = k_eff - c_hi

    ti = lax.broadcasted_iota(jnp.int32, (tk, tk), 0)
    tj = lax.broadcasted_iota(jnp.int32, (tk, tk), 1)
    tri_incl = jnp.where(tj <= ti, 1.0, 0.0).astype(BF16)
    neg = -1e30

    def attend(b):
        thr_b, n_tie_b = thr[:, seq_lanes[b]], n_tie[:, seq_lanes[b]]

        def logit_tile(t, carry):
            seen, pmax = carry
            s = sc_ref[t, :, seq_lanes[b]]
            tie = s == thr_b
            incl = jnp.dot(tri_incl, jnp.where(tie, 1.0, 0.0).astype(BF16), preferred_element_type=F32)
            sel = jnp.logical_or(s > thr_b, jnp.logical_and(tie, incl <= n_tie_b - seen))
            bias = jnp.where(sel, 0.0, neg)
            kt = key_rows(k_ref, b, t)
            pmax_out = []
            for h in range(DSA_HEADS):
                lg = lax.dot_general(kt, q_ref[b, :, h * LANES:(h + 1) * LANES], nt,
                                     preferred_element_type=F32) + bias
                lg_ref[t, h] = lg
                pmax_out.append(jnp.maximum(pmax[h], fold_max(lg)))
            return seen + incl[tk - 1:tk, :], tuple(pmax_out)

        pmax0 = tuple(jnp.full((8, qb), neg, F32) for _ in range(DSA_HEADS))
        _, pmax = for_each_tile(logit_tile, (jnp.zeros((1, qb), F32), pmax0), DSA_UNROLL)
        m = [jnp.max(pmax[h], axis=0, keepdims=True) for h in range(DSA_HEADS)]

        acc_ref[...] = jnp.zeros_like(acc_ref)

        def value_tiles(tiles):
            vt = jnp.concatenate([vt_ref[b, t] for t in tiles], axis=1)
            for h in range(DSA_HEADS):
                lanes = slice(h * qb, (h + 1) * qb)
                p = jnp.concatenate([jnp.exp2((lg_ref[t, h] - m[h]).astype(BF16)) for t in tiles], axis=0)
                acc_ref[:, lanes] += jnp.dot(vt, p, preferred_element_type=F32)

        @pl.loop(0, n_tiles // DSA_UNROLL)
        def _(i):
            value_tiles([DSA_UNROLL * i + k for k in range(DSA_UNROLL)])

        first = (n_tiles // DSA_UNROLL) * DSA_UNROLL
        tail = DSA_UNROLL // 2
        while tail >= 1:
            pl.when((n_tiles & tail) != 0)(functools.partial(value_tiles, [first + k for k in range(tail)]))
            first = first + (n_tiles & tail)
            tail //= 2

        for h in range(DSA_HEADS):
            lanes = slice(h * qb, (h + 1) * qb)
            out_t = acc_ref[:DSA_HEAD_DIM, lanes] / acc_ref[DSA_HEAD_DIM:DSA_HEAD_DIM + 1, lanes]
            o_ref[b, :, h * LANES:(h + 1) * LANES] = out_t.T.astype(BF16)

    for b in range(nb):
        attend(b)


def _dsa(iq, misc, dq, ik2, dk, dvt, batch, seq):
    qb, tk = DSA_QB, DSA_TK
    assert seq % qb == 0 and seq % tk == 0 and qb % CHUNK == 0
    nq = seq // qb
    nkt = seq // tk
    n_sel = min(TOPK_MAX, seq // 4)

    nb = DSA_SEQS if batch % DSA_SEQS == 0 else 1

    def per_seq(a):
        return a.reshape((batch, seq) + a.shape[1:])

    def qrow(b, j):
        return (b, j, 0)

    def kv(b, j):
        return (b, 0, 0)

    out = pl.pallas_call(
        functools.partial(_dsa_kernel, n_sel=n_sel),
        grid=(batch // nb, nq),
        in_specs=[
            pl.BlockSpec((nb, qb, IDX_Q), qrow),
            pl.BlockSpec((nb, qb, LANES), qrow),
            pl.BlockSpec((nb, qb, DSA_Q), qrow),
            pl.BlockSpec((nb, seq, LANES), kv),
            pl.BlockSpec((nb, seq, DSA_HEAD_DIM), kv),
            pl.BlockSpec((nb, nkt, DSA_VROWS, tk), lambda b, j: (b, 0, 0, 0)),
        ],
        out_specs=pl.BlockSpec((nb, qb, DSA_Q), qrow),
        out_shape=jax.ShapeDtypeStruct((batch, seq, DSA_Q), BF16),
        scratch_shapes=[pltpu.VMEM((nkt, tk, nb * qb), F32),
                        pltpu.VMEM((DSA_VROWS, DSA_HEADS * qb), F32),
                        pltpu.VMEM((nb, IDX_HEADS, qb, LANES), BF16),
                        pltpu.VMEM((nkt, DSA_HEADS, tk, qb), F32),
                        pltpu.VMEM((nkt, tk, nb * qb), BF16)],
        compiler_params=pltpu.CompilerParams(dimension_semantics=("parallel", "arbitrary"),
                                             vmem_limit_bytes=VMEM_LIMIT),
        name="dsa",
    )(per_seq(iq), per_seq(misc), per_seq(dq), per_seq(ik2), per_seq(dk),
      dvt.reshape((batch, nkt) + dvt.shape[1:]))
    return out.reshape(batch * seq, DSA_Q)


def _out_ffn_kernel(gla_ref, dsa_ref, x_ref, wo_ref, n1_ref, n2_ref, w1_ref, w2_ref, n3_ref, o_ref):
    mixed = (jnp.dot(gla_ref[...], wo_ref[:GLA_V, :], preferred_element_type=F32)
             + jnp.dot(dsa_ref[...], wo_ref[GLA_V:, :], preferred_element_type=F32))
    x1 = x_ref[...] + _rms(mixed, n1_ref[...])
    h = _rms(x1, n2_ref[...]).astype(BF16)
    d_ff = w1_ref.shape[1]
    acc = jnp.zeros(x1.shape, F32)
    for f in range(d_ff // FFN_TF):
        cols = slice(f * FFN_TF, (f + 1) * FFN_TF)
        a = jnp.maximum(jnp.dot(h, w1_ref[:, cols], preferred_element_type=F32), 0.0)
        acc = acc + jnp.dot((a * a).astype(BF16), w2_ref[cols, :], preferred_element_type=F32)
    o_ref[...] = x1 + _rms(acc, n3_ref[...])


def _out_ffn(gla, dsa, x2, wo, n1, n2, w1, w2, n3):
    t, d = x2.shape
    tm = FFN_TM
    d_ff = w1.shape[1]
    assert t % tm == 0 and d_ff % FFN_TF == 0

    def row(i):
        return (i, 0)

    def const(i):
        return (0, 0)

    def resident(shape):
        return pl.BlockSpec(shape, const, pipeline_mode=pl.Buffered(1))

    return pl.pallas_call(
        _out_ffn_kernel,
        grid=(t // tm,),
        in_specs=[
            pl.BlockSpec((tm, GLA_V), row),
            pl.BlockSpec((tm, DSA_Q), row),
            pl.BlockSpec((tm, d), row),
            resident((GLA_V + DSA_Q, d)),
            resident((1, d)),
            resident((1, d)),
            resident((d, d_ff)),
            resident((d_ff, d)),
            resident((1, d)),
        ],
        out_specs=pl.BlockSpec((tm, d), row),
        out_shape=jax.ShapeDtypeStruct((t, d), F32),
        compiler_params=pltpu.CompilerParams(dimension_semantics=("parallel",), vmem_limit_bytes=VMEM_LIMIT),
        name="out_ffn",
    )(gla, dsa, x2, wo, n1, n2, w1, w2, n3)


def _rope_tables(seq):
    pos = jnp.arange(seq, dtype=F32)[:, None]

    def table(dim, reps):
        inv = ROPE_THETA ** (-jnp.arange(0, dim, 2, dtype=F32) / dim)
        ang = pos * inv[None, :]
        cos, sin = jnp.cos(ang), jnp.sin(ang)
        return (jnp.tile(jnp.concatenate([cos, cos], axis=-1), (1, reps)),
                jnp.tile(jnp.concatenate([-sin, sin], axis=-1), (1, reps)))

    cos128, sin128 = table(DSA_HEAD_DIM, 1)
    cos64, sin64 = table(IDX_DIM, LANES // IDX_DIM)
    return cos128, sin128, cos64, sin64


def _pack_w_in(w_in):
    sizes = (GLA_QK, GLA_QK, GLA_V, GLA_GATE_RANK, GLA_V, DSA_Q, DSA_HEAD_DIM, DSA_HEAD_DIM, IDX_Q, IDX_DIM, IDX_HEADS)
    offs = np.concatenate([[0], np.cumsum(sizes)])
    g_q, g_k, g_v, g_lr, g_r, d_q, d_k, d_v, i_q, i_k, i_w = [w_in[:, offs[i]:offs[i + 1]] for i in range(len(sizes))]
    pad = jnp.zeros((w_in.shape[0], LANES - IDX_DIM - IDX_HEADS - GLA_GATE_RANK), w_in.dtype)
    return jnp.concatenate([g_q, g_k, g_v, g_r, d_q, d_k, d_v, i_q, i_k, i_w, g_lr, pad], axis=1)


def kernel(x, norm_mix_pre, w_in, gla_gate_w2, gla_gate_b, gla_norm_w, idx_k_norm_w, idx_k_norm_b,
           w_out, norm_mix_post, norm_ffn_pre, w_ff1, w_ff2, norm_ffn_post):
    batch, seq, d = x.shape
    tabs = _rope_tables(seq)
    x2 = x.reshape(batch * seq, d)
    for l in range(w_in.shape[0]):
        w_pack = _pack_w_in(w_in[l]).astype(BF16)
        w2_pad = jnp.zeros((LANES, GLA_QK), F32).at[M_LR:M_LR + GLA_GATE_RANK].set(gla_gate_w2[l]).astype(BF16)
        lnw = jnp.zeros((1, LANES), F32).at[0, :IDX_DIM].set(idx_k_norm_w[l])
        lnb = jnp.zeros((1, LANES), F32).at[0, :IDX_DIM].set(idx_k_norm_b[l])
        (gla, dq, dk, dvt, iq, ik2, misc) = _projection(
            x2, norm_mix_pre[l][None], w_pack, w2_pad, gla_gate_b[l][None], lnw, lnb, tabs,
            gla_norm_w[l][None], seq)
        dsa = _dsa(iq, misc, dq, ik2, dk, dvt, batch, seq)
        x2 = _out_ffn(gla, dsa, x2, w_out[l].astype(BF16), norm_mix_post[l][None], norm_ffn_pre[l][None],
                      w_ff1[l].astype(BF16), w_ff2[l].astype(BF16), norm_ffn_post[l][None])
    return x2.reshape(batch, seq, d)
```
